```python
import jax, jax.numpy as jnp
from jax import lax
import numpy as np

D_MODEL = 2048
BATCH = 4
SEQ = 2048
DEPTH = 4
DEC_BATCH = 128
DEC_SEQ = 1
PAST_LEN = 8192
PAGE_SIZE = 128

N_MIXERS = 3
CONV_WIDTH = 31
CONV_DIM = D_MODEL
HEAD_DIM = 128
MOBA_Q_HEADS = D_MODEL // HEAD_DIM
MOBA_KV_HEADS = 4
MOBA_GROUP = MOBA_Q_HEADS // MOBA_KV_HEADS
MOBA_BLOCK = 256
MOBA_TOPK = 3
MOBA_QCHUNK = 16
MLA_HEADS = D_MODEL // 128
MLA_Q_LORA = D_MODEL // 4
MLA_KV_LORA = D_MODEL // 4
MLA_NOPE = 128
MLA_ROPE = 64
MLA_V = 128
MLA_QBLOCK = 128
D_FF = 4 * D_MODEL
ROPE_THETA = 10000.0
LN_EPS = 1e-5
RMS_EPS = 1e-6
DEEPNORM_ALPHA = (2 * DEPTH) ** 0.25
DEEPNORM_BETA = (8 * DEPTH) ** -0.25
N_CONV_LAYERS = len(range(0, DEPTH, N_MIXERS))
N_MOBA_LAYERS = len(range(1, DEPTH, N_MIXERS))
N_MLA_LAYERS = len(range(2, DEPTH, N_MIXERS))

kernel_name = 'hybrid_conv_moba_mla_deepnorm_step'


def layer_norm(x, g, b):
    xf = x.astype(jnp.float32)
    mu = jnp.mean(xf, -1, keepdims=True)
    var = jnp.mean(jnp.square(xf - mu), -1, keepdims=True)
    return ((xf - mu) * lax.rsqrt(var + LN_EPS) * g.astype(jnp.float32) + b.astype(jnp.float32)).astype(x.dtype)


def rms_norm(x, g):
    xf = x.astype(jnp.float32)
    return (xf * lax.rsqrt(jnp.mean(jnp.square(xf), -1, keepdims=True) + RMS_EPS) * g.astype(jnp.float32)).astype(x.dtype)


def rope_tables(pos, dim):
    inv = ROPE_THETA ** (-jnp.arange(0, dim, 2, dtype=jnp.float32) / dim)
    ang = pos.astype(jnp.float32)[:, None] * inv[None, :]
    ang = jnp.concatenate([ang, ang], axis=-1)
    return jnp.cos(ang), jnp.sin(ang)


def apply_rope(x, cos, sin):
    xf = x.astype(jnp.float32)
    x1, x2 = jnp.split(xf, 2, axis=-1)
    return (xf * cos + jnp.concatenate([-x2, x1], -1) * sin).astype(x.dtype)


def sq_relu_mlp(x, w1, w2):
    return jnp.square(jax.nn.relu(x @ w1)) @ w2


def conv_mixer(x, prev, w_in, b_in, w_dw, b_dw, g, b, w_out, b_out):
    a, gate = jnp.split(x @ w_in + b_in, 2, axis=-1)
    u = a * jax.nn.sigmoid(gate)
    up = jnp.concatenate([prev.astype(u.dtype), u], axis=1)
    c = lax.conv_general_dilated(up, w_dw[:, None, :].astype(up.dtype), window_strides=(1,), padding='VALID',
                                 dimension_numbers=('NWC', 'WIO', 'NWC'), feature_group_count=CONV_DIM) + b_dw
    c = jax.nn.silu(layer_norm(c, g, b))
    return c @ w_out + b_out, up[:, up.shape[1] - (CONV_WIDTH - 1):]


def moba_project(x, pos, w_qkv):
    B, T, _ = x.shape
    nq, nk = MOBA_Q_HEADS * HEAD_DIM, MOBA_KV_HEADS * HEAD_DIM
    qkv = x @ w_qkv
    q = qkv[..., :nq].reshape(B, T, MOBA_Q_HEADS, HEAD_DIM)
    k = qkv[..., nq:nq + nk].reshape(B, T, MOBA_KV_HEADS, HEAD_DIM)
    v = qkv[..., nq + nk:].reshape(B, T, MOBA_KV_HEADS, HEAD_DIM)
    cos, sin = rope_tables(pos, HEAD_DIM)
    cos, sin = cos[:, None, :], sin[:, None, :]
    return apply_rope(q, cos, sin), apply_rope(k, cos, sin), v


def moba_core(q, q_pos, block_mean, gather_kv):
    B, Tq = q.shape[0], q.shape[1]
    nbk = block_mean.shape[1]
    topk = min(MOBA_TOPK, nbk)
    scale = HEAD_DIM ** -0.5
    offs = jnp.arange(MOBA_BLOCK)
    own = q_pos // MOBA_BLOCK
    qg = q.reshape(B, Tq, MOBA_KV_HEADS, MOBA_GROUP, HEAD_DIM)
    gate = jnp.einsum('btkgd,bnkd->btkgn', qg.astype(jnp.float32), block_mean).reshape(B, Tq, MOBA_Q_HEADS, nbk)
    past = jnp.arange(nbk)[None, :] < own[:, None]
    gate = jnp.where(past[None, :, None, :], gate, -jnp.inf)
    _, sel = lax.top_k(gate, topk)
    sel_ok = sel < own[None, :, None, None]
    sel_pos = (sel[..., None] * MOBA_BLOCK + offs).reshape(B, Tq, MOBA_Q_HEADS, topk * MOBA_BLOCK)
    q_kv_head = (jnp.arange(MOBA_Q_HEADS) // MOBA_GROUP)[None, None, :, None]
    k_sel, v_sel = gather_kv(sel_pos, q_kv_head)
    s_sel = jnp.einsum('bthd,bthnd->bthn', q, k_sel, preferred_element_type=jnp.float32) * scale
    s_sel = jnp.where(jnp.repeat(sel_ok, MOBA_BLOCK, axis=-1), s_sel, -jnp.inf)
    own_pos = own[:, None] * MOBA_BLOCK + offs
    own_ok = own_pos <= q_pos[:, None]
    k_own, v_own = gather_kv(own_pos[None, :, None, :], jnp.arange(MOBA_KV_HEADS)[None, None, :, None])
    s_own = jnp.einsum('btkgd,btknd->btkgn', qg, k_own, preferred_element_type=jnp.float32).reshape(B, Tq, MOBA_Q_HEADS, MOBA_BLOCK) * scale
    s_own = jnp.where(own_ok[None, :, None, :], s_own, -jnp.inf)
    p = jax.nn.softmax(jnp.concatenate([s_sel, s_own], axis=-1), axis=-1)
    n_sel = topk * MOBA_BLOCK
    p_sel = p[..., :n_sel].astype(v_sel.dtype)
    p_own = p[..., n_sel:].reshape(B, Tq, MOBA_KV_HEADS, MOBA_GROUP, MOBA_BLOCK).astype(v_own.dtype)
    o = jnp.einsum('bthn,bthnd->bthd', p_sel, v_sel, preferred_element_type=jnp.float32)
    o = o + jnp.einsum('btkgn,btknd->btkgd', p_own, v_own, preferred_element_type=jnp.float32).reshape(B, Tq, MOBA_Q_HEADS, HEAD_DIM)
    return o.astype(q.dtype)


def moba_prompt(q, k, v):
    B, S = q.shape[0], q.shape[1]
    nbk = -(-S // MOBA_BLOCK)
    kp = jnp.pad(k.astype(jnp.float32), ((0, 0), (0, nbk * MOBA_BLOCK - S), (0, 0), (0, 0)))
    means = kp.reshape(B, nbk, MOBA_BLOCK, MOBA_KV_HEADS, HEAD_DIM).sum(2) / MOBA_BLOCK
    b_idx = jnp.arange(B)[:, None, None, None]

    def gather(pos, hk):
        pc = jnp.clip(pos, 0, S - 1)
        return k[b_idx, pc, hk], v[b_idx, pc, hk]

    def chunk(start):
        qc = lax.dynamic_slice_in_dim(q, start, MOBA_QCHUNK, axis=1)
        return moba_core(qc, start + jnp.arange(MOBA_QCHUNK), means, gather)

    out = lax.map(chunk, jnp.arange(S // MOBA_QCHUNK) * MOBA_QCHUNK)
    return jnp.moveaxis(out, 0, 1).reshape(B, S, MOBA_Q_HEADS, HEAD_DIM)


def moba_sample(q, k_new, v_new, cache_k, cache_v, j, page_table):
    T = q.shape[1]
    n_pages = page_table.shape[1]
    nbk = (PAST_LEN + T + MOBA_BLOCK - 1) // MOBA_BLOCK
    page_blk = (jnp.arange(n_pages) * PAGE_SIZE) // MOBA_BLOCK
    new_blk = (PAST_LEN + jnp.arange(T)) // MOBA_BLOCK
    q_pos = PAST_LEN + jnp.arange(T)

    def one_sequence(args):
        qb, kb, vb, ptab = args
        page_sum = cache_k[j, ptab].astype(jnp.float32).sum(axis=1)
        bsum = jnp.zeros((nbk, MOBA_KV_HEADS, HEAD_DIM), jnp.float32).at[page_blk].add(page_sum).at[new_blk].add(kb.astype(jnp.float32))
        means = (bsum / MOBA_BLOCK)[None]

        def gather(pos, hk):
            in_past = (pos < PAST_LEN)[..., None]
            pc = jnp.clip(pos, 0, PAST_LEN - 1)
            phys = ptab[pc // PAGE_SIZE]
            off = pc % PAGE_SIZE
            nc = jnp.clip(pos - PAST_LEN, 0, T - 1)
            return (jnp.where(in_past, cache_k[j, phys, off, hk], kb[nc, hk]),
                    jnp.where(in_past, cache_v[j, phys, off, hk], vb[nc, hk]))

        return moba_core(qb[None], q_pos, means, gather)[0]

    return lax.map(one_sequence, (q, k_new, v_new, page_table))


def mla_project(x, pos, w_in, q_norm, w_uq, kv_norm):
    B, T, _ = x.shape
    h = x @ w_in
    cq = rms_norm(h[..., :MLA_Q_LORA], q_norm)
    ckv = rms_norm(h[..., MLA_Q_LORA:MLA_Q_LORA + MLA_KV_LORA], kv_norm)
    kr = h[..., MLA_Q_LORA + MLA_KV_LORA:]
    q = (cq @ w_uq).reshape(B, T, MLA_HEADS, MLA_NOPE + MLA_ROPE)
    cos, sin = rope_tables(pos, MLA_ROPE)
    qr = apply_rope(q[..., MLA_NOPE:], cos[:, None, :], sin[:, None, :])
    kr = apply_rope(kr, cos, sin)
    return q[..., :MLA_NOPE], qr, ckv, kr


def mla_prompt_attend(qn, qr, ckv, kr, w_uk, w_uv):
    B, S = qn.shape[0], qn.shape[1]
    k_nope = jnp.einsum('bsc,chn->bshn', ckv, w_uk)
    v = jnp.einsum('bsc,chv->bshv', ckv, w_uv)
    kpos = jnp.arange(S)
    scale = (MLA_NOPE + MLA_ROPE) ** -0.5

    def block(start):
        qnb = lax.dynamic_slice_in_dim(qn, start, MLA_QBLOCK, axis=1)
        qrb = lax.dynamic_slice_in_dim(qr, start, MLA_QBLOCK, axis=1)
        s = (jnp.einsum('bthn,bshn->bhts', qnb, k_nope, preferred_element_type=jnp.float32)
             + jnp.einsum('bthr,bsr->bhts', qrb, kr, preferred_element_type=jnp.float32)) * scale
        qpos = start + jnp.arange(MLA_QBLOCK)
        s = jnp.where(kpos[None, :] <= qpos[:, None], s, -jnp.inf)
        p = jax.nn.softmax(s, axis=-1).astype(v.dtype)
        return jnp.einsum('bhts,bshv->bthv', p, v, preferred_element_type=jnp.float32).astype(qn.dtype)

    out = lax.map(block, jnp.arange(S // MLA_QBLOCK) * MLA_QBLOCK)
    return jnp.moveaxis(out, 0, 1).reshape(B, S, MLA_HEADS, MLA_V)


def mla_sample_attend(qn, qr, ckv_new, kr_new, lat_cache, rope_cache, j, page_table, w_uk, w_uv):
    T = qn.shape[1]
    scale = (MLA_NOPE + MLA_ROPE) ** -0.5
    q_lat = jnp.einsum('bthn,chn->bthc', qn, w_uk, preferred_element_type=jnp.float32)
    qrf = qr.astype(jnp.float32)

    def scores(c, r):
        return (jnp.einsum('bthc,blc->bhtl', q_lat, c.astype(jnp.float32))
                + jnp.einsum('bthr,blr->bhtl', qrf, r.astype(jnp.float32))) * scale

    tpos = jnp.arange(T)
    s = jnp.where(tpos[None, :] <= tpos[:, None], scores(ckv_new, kr_new), -jnp.inf)
    m = s.max(-1)
    p = jnp.exp(s - m[..., None])
    l = p.sum(-1)
    acc = jnp.einsum('bhtl,blc->bhtc', p, ckv_new.astype(jnp.float32))

    def step(carry, pcol):
        m, l, acc = carry
        c = lat_cache[j, pcol]
        r = rope_cache[j, pcol]
        s = scores(c, r)
        m_new = jnp.maximum(m, s.max(-1))
        corr = jnp.exp(m - m_new)
        p = jnp.exp(s - m_new[..., None])
        return (m_new, l * corr + p.sum(-1), acc * corr[..., None] + jnp.einsum('bhtl,blc->bhtc', p, c.astype(jnp.float32))), None

    (m, l, acc), _ = lax.scan(step, (m, l, acc), page_table.T)
    o_lat = acc / l[..., None]
    return jnp.einsum('bhtc,chv->bthv', o_lat, w_uv.astype(jnp.float32)).astype(qn.dtype)


def setup_inputs(seed: int = 0) -> dict:
    key = jax.random.key(seed)
    ks = list(jax.random.split(key, 32))

    def nrm(i, shape, scale=1.0):
        return jax.random.normal(ks[i], shape, jnp.float32) * scale

    n_pages = PAST_LEN // PAGE_SIZE
    n_pool = (DEC_BATCH * n_pages * 5) // 4
    page_table = jax.random.permutation(ks[7], n_pool)[:DEC_BATCH * n_pages].reshape(DEC_BATCH, n_pages).astype(jnp.int32)
    d = D_MODEL
    return {
        'x_prompt': nrm(0, (BATCH, SEQ, d)),
        'x_sample': nrm(1, (DEC_BATCH, DEC_SEQ, d)),
        'state_conv': nrm(2, (N_CONV_LAYERS, DEC_BATCH, CONV_WIDTH - 1, CONV_DIM), 0.5),
        'cache_moba_k': nrm(3, (N_MOBA_LAYERS, n_pool, PAGE_SIZE, MOBA_KV_HEADS, HEAD_DIM)),
        'cache_moba_v': nrm(4, (N_MOBA_LAYERS, n_pool, PAGE_SIZE, MOBA_KV_HEADS, HEAD_DIM)),
        'cache_mla_latent': nrm(5, (N_MLA_LAYERS, n_pool, PAGE_SIZE, MLA_KV_LORA)),
        'cache_mla_krope': nrm(6, (N_MLA_LAYERS, n_pool, PAGE_SIZE, MLA_ROPE)),
        'page_table': page_table,
        'conv_w_in': nrm(8, (N_CONV_LAYERS, d, 2 * CONV_DIM), d ** -0.5),
        'conv_b_in': nrm(9, (N_CONV_LAYERS, 2 * CONV_DIM), 0.02),
        'conv_w_dw': nrm(10, (N_CONV_LAYERS, CONV_WIDTH, CONV_DIM), CONV_WIDTH ** -0.5),
        'conv_b_dw': nrm(11, (N_CONV_LAYERS, CONV_DIM), 0.02),
        'conv_ln_g': 1.0 + nrm(12, (N_CONV_LAYERS, CONV_DIM), 0.02),
        'conv_ln_b': nrm(13, (N_CONV_LAYERS, CONV_DIM), 0.02),
        'conv_w_out': nrm(14, (N_CONV_LAYERS, CONV_DIM, d), CONV_DIM ** -0.5 * DEEPNORM_BETA),
        'conv_b_out': nrm(15, (N_CONV_LAYERS, d), 0.02),
        'moba_w_qkv': nrm(16, (N_MOBA_LAYERS, d, (MOBA_Q_HEADS + 2 * MOBA_KV_HEADS) * HEAD_DIM), d ** -0.5),
        'moba_w_o': nrm(17, (N_MOBA_LAYERS, MOBA_Q_HEADS * HEAD_DIM, d), (MOBA_Q_HEADS * HEAD_DIM) ** -0.5 * DEEPNORM_BETA),
        'mla_w_in': nrm(18, (N_MLA_LAYERS, d, MLA_Q_LORA + MLA_KV_LORA + MLA_ROPE), d ** -0.5),
        'mla_q_norm': 1.0 + nrm(19, (N_MLA_LAYERS, MLA_Q_LORA), 0.02),
        'mla_w_uq': nrm(20, (N_MLA_LAYERS, MLA_Q_LORA, MLA_HEADS * (MLA_NOPE + MLA_ROPE)), MLA_Q_LORA ** -0.5),
        'mla_kv_norm': 1.0 + nrm(21, (N_MLA_LAYERS, MLA_KV_LORA), 0.02),
        'mla_w_uk': nrm(22, (N_MLA_LAYERS, MLA_KV_LORA, MLA_HEADS, MLA_NOPE), MLA_KV_LORA ** -0.5),
        'mla_w_uv': nrm(23, (N_MLA_LAYERS, MLA_KV_LORA, MLA_HEADS, MLA_V), MLA_KV_LORA ** -0.5),
        'mla_w_o': nrm(24, (N_MLA_LAYERS, MLA_HEADS * MLA_V, d), (MLA_HEADS * MLA_V) ** -0.5 * DEEPNORM_BETA),
        'mlp_w1': nrm(25, (DEPTH, d, D_FF), d ** -0.5),
        'mlp_w2': nrm(26, (DEPTH, D_FF, d), D_FF ** -0.5 * DEEPNORM_BETA),
        'ln_g': 1.0 + nrm(27, (DEPTH, 2, d), 0.02),
        'ln_b': nrm(28, (DEPTH, 2, d), 0.02),
    }


def reference(x_prompt, x_sample, state_conv, cache_moba_k, cache_moba_v, cache_mla_latent, cache_mla_krope, page_table,
              conv_w_in, conv_b_in, conv_w_dw, conv_b_dw, conv_ln_g, conv_ln_b, conv_w_out, conv_b_out,
              moba_w_qkv, moba_w_o,
              mla_w_in, mla_q_norm, mla_w_uq, mla_kv_norm, mla_w_uk, mla_w_uv, mla_w_o,
              mlp_w1, mlp_w2, ln_g, ln_b):
    pos_p = jnp.arange(x_prompt.shape[1], dtype=jnp.int32)
    pos_s = PAST_LEN + jnp.arange(x_sample.shape[1], dtype=jnp.int32)
    hp, hs = x_prompt, x_sample
    Bp, Tp = hp.shape[0], hp.shape[1]
    Bs, Ts = hs.shape[0], hs.shape[1]
    conv_p, conv_s = [], []
    mk_p, mv_p, mk_s, mv_s = [], [], [], []
    ml_p, mr_p, ml_s, mr_s = [], [], [], []
    for i in range(DEPTH):
        kind, j = i % N_MIXERS, i // N_MIXERS
        if kind == 0:
            cw = (conv_w_in[j], conv_b_in[j], conv_w_dw[j], conv_b_dw[j], conv_ln_g[j], conv_ln_b[j], conv_w_out[j], conv_b_out[j])
            yp, st_p = conv_mixer(hp, jnp.zeros((Bp, CONV_WIDTH - 1, CONV_DIM), hp.dtype), *cw)
            ys, st_s = conv_mixer(hs, state_conv[j], *cw)
            conv_p.append(st_p)
            conv_s.append(st_s)
        elif kind == 1:
            qp, kp, vp = moba_project(hp, pos_p, moba_w_qkv[j])
            qs, ks_, vs = moba_project(hs, pos_s, moba_w_qkv[j])
            op = moba_prompt(qp, kp, vp)
            os_ = moba_sample(qs, ks_, vs, cache_moba_k, cache_moba_v, j, page_table)
            yp = op.reshape(Bp, Tp, MOBA_Q_HEADS * HEAD_DIM) @ moba_w_o[j]
            ys = os_.reshape(Bs, Ts, MOBA_Q_HEADS * HEAD_DIM) @ moba_w_o[j]
            mk_p.append(kp)
            mv_p.append(vp)
            mk_s.append(ks_)
            mv_s.append(vs)
        else:
            qn_p, qr_p, c_p, r_p = mla_project(hp, pos_p, mla_w_in[j], mla_q_norm[j], mla_w_uq[j], mla_kv_norm[j])
            qn_s, qr_s, c_s, r_s = mla_project(hs, pos_s, mla_w_in[j], mla_q_norm[j], mla_w_uq[j], mla_kv_norm[j])
            op = mla_prompt_attend(qn_p, qr_p, c_p, r_p, mla_w_uk[j], mla_w_uv[j])
            os_ = mla_sample_attend(qn_s, qr_s, c_s, r_s, cache_mla_latent, cache_mla_krope, j, page_table, mla_w_uk[j], mla_w_uv[j])
            yp = op.reshape(Bp, Tp, MLA_HEADS * MLA_V) @ mla_w_o[j]
            ys = os_.reshape(Bs, Ts, MLA_HEADS * MLA_V) @ mla_w_o[j]
            ml_p.append(c_p)
            mr_p.append(r_p)
            ml_s.append(c_s)
            mr_s.append(r_s)
        hp = layer_norm(DEEPNORM_ALPHA * hp + yp, ln_g[i, 0], ln_b[i, 0])
        hs = layer_norm(DEEPNORM_ALPHA * hs + ys, ln_g[i, 0], ln_b[i, 0])
        hp = layer_norm(DEEPNORM_ALPHA * hp + sq_relu_mlp(hp, mlp_w1[i], mlp_w2[i]), ln_g[i, 1], ln_b[i, 1])
        hs = layer_norm(DEEPNORM_ALPHA * hs + sq_relu_mlp(hs, mlp_w1[i], mlp_w2[i]), ln_g[i, 1], ln_b[i, 1])
    return (hp, hs,
            jnp.stack(conv_p), jnp.stack(conv_s),
            jnp.stack(mk_p), jnp.stack(mv_p), jnp.stack(mk_s), jnp.stack(mv_s),
            jnp.stack(ml_p), jnp.stack(mr_p), jnp.stack(ml_s), jnp.stack(mr_s))
```

```python
import functools
import math

import numpy as np
import jax
import jax.numpy as jnp
from jax import lax
from jax.experimental import pallas as pl
from jax.experimental.pallas import tpu as pltpu

F32 = jnp.float32
BF16 = jnp.bfloat16

MOBA_BLOCK = 256
MOBA_TOPK = 3
ROPE_THETA = 10000.0
LN_EPS = 1e-5
RMS_EPS = 1e-6
MASKED = -1e30

LANES = 128
SUBLANES = 8
VMEM_CAP_BYTES = 60 * 1024 * 1024
VMEM_FLOOR_BYTES = 32 * 1024 * 1024

NT_DIMS = (((1,), (1,)), ((), ()))


def _divisor_tile(n, target, mult):
    best = None
    for t in range(mult, min(n, target) + 1, mult):
        if n % t == 0:
            best = t
    return n if best is None else best


def _params(semantics, block_bytes):
    limit = int(min(max(block_bytes * 1.3 + (6 << 20), VMEM_FLOOR_BYTES), VMEM_CAP_BYTES))
    return pltpu.CompilerParams(dimension_semantics=semantics, vmem_limit_bytes=limit)


def _nbytes(shape, dtype):
    return int(np.prod(shape)) * jnp.dtype(dtype).itemsize


def _layer_norm(z, g, b):
    mu = jnp.mean(z, axis=-1, keepdims=True)
    zc = z - mu
    var = jnp.mean(zc * zc, axis=-1, keepdims=True)
    return zc * lax.rsqrt(var + LN_EPS) * g + b


def _sigmoid(x):
    return 1.0 / (1.0 + jnp.exp(-x))


def _mm_kernel(x_ref, w_ref, o_ref):
    o_ref[...] = jnp.dot(x_ref[...], w_ref[...], preferred_element_type=F32).astype(o_ref.dtype)


def _matmul(x, w, out_dtype, rows=None, tm_target=1024, tn_target=512):
    rows = x.shape[0] if rows is None else rows
    k, nout = w.shape
    tm = _divisor_tile(rows, tm_target, 16)
    tn = _divisor_tile(nout, tn_target, LANES)
    blocks = 2 * (_nbytes((tm, k), BF16) + _nbytes((k, tn), BF16) + _nbytes((tm, tn), out_dtype)) + _nbytes((tm, tn), F32)
    return pl.pallas_call(
        _mm_kernel,
        grid=(rows // tm, nout // tn),
        in_specs=[pl.BlockSpec((tm, k), lambda i, j: (i, 0)), pl.BlockSpec((k, tn), lambda i, j: (0, j))],
        out_specs=pl.BlockSpec((tm, tn), lambda i, j: (i, j)),
        out_shape=jax.ShapeDtypeStruct((rows, nout), out_dtype),
        compiler_params=_params(("parallel", "parallel"), blocks),
        name="matmul",
    )(x, w)


def _glu_kernel(x_ref, wa_ref, wg_ref, ba_ref, bg_ref, o_ref):
    x = x_ref[...]
    a = jnp.dot(x, wa_ref[...], preferred_element_type=F32) + ba_ref[...]
    g = jnp.dot(x, wg_ref[...], preferred_element_type=F32) + bg_ref[...]
    o_ref[...] = a * _sigmoid(g)


def _glu_matmul(x, w, b):
    n, k = x.shape
    c = w.shape[1] // 2
    tm = _divisor_tile(n, 1024, 16)
    tn = _divisor_tile(c, 512, LANES)
    nj = c // tn
    blocks = 2 * (_nbytes((tm, k), BF16) + 2 * _nbytes((k, tn), BF16) + _nbytes((tm, tn), F32)) + 3 * _nbytes((tm, tn), F32)
    return pl.pallas_call(
        _glu_kernel,
        grid=(n // tm, nj),
        in_specs=[
            pl.BlockSpec((tm, k), lambda i, j: (i, 0)),
            pl.BlockSpec((k, tn), lambda i, j: (0, j)),
            pl.BlockSpec((k, tn), lambda i, j: (0, nj + j)),
            pl.BlockSpec((1, tn), lambda i, j: (0, j)),
            pl.BlockSpec((1, tn), lambda i, j: (0, nj + j)),
        ],
        out_specs=pl.BlockSpec((tm, tn), lambda i, j: (i, j)),
        out_shape=jax.ShapeDtypeStruct((n, c), F32),
        compiler_params=_params(("parallel", "parallel"), blocks),
        name="glu_matmul",
    )(x, w, w, b, b)


def _rotate_half(a, half):
    if 2 * half == LANES:
        return pltpu.roll(a, half, 1)
    lane = lax.broadcasted_iota(jnp.int32, a.shape, 1)
    return jnp.where(lane % (2 * half) < half, pltpu.roll(a, LANES - half, 1), pltpu.roll(a, half, 1))


def _mm_rope_kernel(x_ref, w_ref, cos_ref, sin_ref, o_ref, *, rope_lo, rope_hi, half):
    acc = jnp.dot(x_ref[...], w_ref[...], preferred_element_type=F32)
    j = pl.program_id(1)
    in_rope = jnp.logical_and(j >= rope_lo, j < rope_hi)

    @pl.when(in_rope)
    def _():
        cos = cos_ref[...]
        sin = sin_ref[...]
        for c0 in range(0, acc.shape[1], LANES):
            a = acc[:, c0:c0 + LANES]
            o_ref[:, c0:c0 + LANES] = (a * cos + _rotate_half(a, half) * sin).astype(o_ref.dtype)

    @pl.when(jnp.logical_not(in_rope))
    def _():
        o_ref[...] = acc.astype(o_ref.dtype)


def _matmul_rope(x, w, cos, sin_signed, out_dtype, rope_cols, half, tn):
    n, k = x.shape
    nout = w.shape[1]
    tm = _divisor_tile(n, 1024, 16)
    assert nout % tn == 0 and rope_cols[0] % tn == 0 and rope_cols[1] % tn == 0
    blocks = 2 * (_nbytes((tm, k), BF16) + _nbytes((k, tn), BF16) + 2 * _nbytes((tm, LANES), F32)
                  + _nbytes((tm, tn), out_dtype)) + 2 * _nbytes((tm, tn), F32)
    kern = functools.partial(_mm_rope_kernel, rope_lo=rope_cols[0] // tn, rope_hi=rope_cols[1] // tn, half=half)
    return pl.pallas_call(
        kern,
        grid=(n // tm, nout // tn),
        in_specs=[
            pl.BlockSpec((tm, k), lambda i, j: (i, 0)),
            pl.BlockSpec((k, tn), lambda i, j: (0, j)),
            pl.BlockSpec((tm, LANES), lambda i, j: (i, 0)),
            pl.BlockSpec((tm, LANES), lambda i, j: (i, 0)),
        ],
        out_specs=pl.BlockSpec((tm, tn), lambda i, j: (i, j)),
        out_shape=jax.ShapeDtypeStruct((n, nout), out_dtype),
        compiler_params=_params(("parallel", "parallel"), blocks),
        name="matmul_rope",
    )(x, w, cos, sin_signed)


def _mm_res_ln_kernel(x_ref, w_ref, bias_ref, res_ref, g_ref, b_ref, of_ref, ob_ref, *, alpha):
    y = jnp.dot(x_ref[...], w_ref[...], preferred_element_type=F32) + bias_ref[...]
    out = _layer_norm(alpha * res_ref[...] + y, g_ref[...], b_ref[...])
    of_ref[...] = out
    ob_ref[...] = out.astype(BF16)


def _matmul_res_ln(x, w, bias, res, g, b, alpha):
    n, k = x.shape
    d = w.shape[1]
    tm = _divisor_tile(n, 640, 16)
    blocks = (2 * (_nbytes((tm, k), BF16) + _nbytes((k, d), BF16) + 2 * _nbytes((tm, d), F32) + _nbytes((tm, d), BF16))
              + 3 * _nbytes((tm, d), F32))
    row = lambda i: (i, 0)
    fixed = lambda i: (0, 0)
    return pl.pallas_call(
        functools.partial(_mm_res_ln_kernel, alpha=alpha),
        grid=(n // tm,),
        in_specs=[
            pl.BlockSpec((tm, k), row), pl.BlockSpec((k, d), fixed), pl.BlockSpec((1, d), fixed),
            pl.BlockSpec((tm, d), row), pl.BlockSpec((1, d), fixed), pl.BlockSpec((1, d), fixed),
        ],
        out_specs=[pl.BlockSpec((tm, d), row), pl.BlockSpec((tm, d), row)],
        out_shape=[jax.ShapeDtypeStruct((n, d), F32), jax.ShapeDtypeStruct((n, d), BF16)],
        compiler_params=_params(("parallel",), blocks),
        name="matmul_res_ln",
    )(x, w, bias, res, g, b)


def _mlp_kernel(xb_ref, xf_ref, w1_ref, w2_ref, g_ref, b_ref, of_ref, ob_ref, acc_ref, *, alpha):
    f = pl.program_id(1)

    @pl.when(f == 0)
    def _():
        acc_ref[...] = jnp.zeros_like(acc_ref)

    h1 = jnp.maximum(jnp.dot(xb_ref[...], w1_ref[...], preferred_element_type=F32), 0.0)
    acc_ref[...] += jnp.dot((h1 * h1).astype(BF16), w2_ref[...], preferred_element_type=F32)

    @pl.when(f == pl.num_programs(1) - 1)
    def _():
        out = _layer_norm(alpha * xf_ref[...] + acc_ref[...], g_ref[...], b_ref[...])
        of_ref[...] = out
        ob_ref[...] = out.astype(BF16)


def _mlp_res_ln(xb, xf, w1, w2, g, b, alpha):
    n, d = xb.shape
    ff = w1.shape[1]
    tm = _divisor_tile(n, 640, 16)
    tf = _divisor_tile(ff, 512, LANES)
    blocks = (2 * (_nbytes((tm, d), BF16) * 2 + 2 * _nbytes((tm, d), F32) + _nbytes((d, tf), BF16) + _nbytes((tf, d), BF16))
              + _nbytes((tm, d), F32) + 2 * _nbytes((tm, tf), F32))
    row = lambda i, f: (i, 0)
    fixed = lambda i, f: (0, 0)
    return pl.pallas_call(
        functools.partial(_mlp_kernel, alpha=alpha),
        grid=(n // tm, ff // tf),
        in_specs=[
            pl.BlockSpec((tm, d), row), pl.BlockSpec((tm, d), row),
            pl.BlockSpec((d, tf), lambda i, f: (0, f)), pl.BlockSpec((tf, d), lambda i, f: (f, 0)),
            pl.BlockSpec((1, d), fixed), pl.BlockSpec((1, d), fixed),
        ],
        out_specs=[pl.BlockSpec((tm, d), row), pl.BlockSpec((tm, d), row)],
        out_shape=[jax.ShapeDtypeStruct((n, d), F32), jax.ShapeDtypeStruct((n, d), BF16)],
        scratch_shapes=[pltpu.VMEM((tm, d), F32)],
        compiler_params=_params(("parallel", "arbitrary"), blocks),
        name="mlp_res_ln",
    )(xb, xf, w1, w2, g, b)


CONV_HALO = 32
CONV_ROWS = 64
CONV_COLS = 256


def _conv_prompt_kernel(halo_ref, u_ref, w_ref, bdw_ref, g_ref, b_ref, o_ref, win_ref, c_ref, *, width, tt):
    first = pl.program_id(1) == 0
    win_ref[0:CONV_HALO, :] = jnp.where(first, 0.0, halo_ref[...])
    win_ref[CONV_HALO:CONV_HALO + tt, :] = u_ref[...]
    chans = u_ref.shape[1]
    off0 = CONV_HALO - (width - 1)
    for r0 in range(0, tt, CONV_ROWS):
        for c0 in range(0, chans, CONV_COLS):
            acc = jnp.zeros((CONV_ROWS, CONV_COLS), F32)
            for k in range(width):
                acc = acc + w_ref[k:k + 1, c0:c0 + CONV_COLS] * win_ref[r0 + off0 + k:r0 + off0 + k + CONV_ROWS, c0:c0 + CONV_COLS]
            c_ref[r0:r0 + CONV_ROWS, c0:c0 + CONV_COLS] = acc + bdw_ref[:, c0:c0 + CONV_COLS]
    y = _layer_norm(c_ref[...], g_ref[...], b_ref[...])
    o_ref[...] = (y * _sigmoid(y)).astype(o_ref.dtype)


def _conv_prompt(u, w_dw, b_dw, g, b, batch, seq):
    chans = u.shape[1]
    width = w_dw.shape[0]
    tt = _divisor_tile(seq, 128, CONV_ROWS)
    assert width - 1 <= CONV_HALO and tt % CONV_HALO == 0 and tt % CONV_ROWS == 0 and chans % CONV_COLS == 0
    nt = seq // tt
    per = tt // CONV_HALO
    blocks = (2 * (_nbytes((CONV_HALO, chans), F32) + _nbytes((tt, chans), F32) + _nbytes((width, chans), F32)
                   + _nbytes((tt, chans), BF16)) + _nbytes((tt + CONV_HALO, chans), F32) + 3 * _nbytes((tt, chans), F32))
    fixed = lambda bi, i: (0, 0)
    return pl.pallas_call(
        functools.partial(_conv_prompt_kernel, width=width, tt=tt),
        grid=(batch, nt),
        in_specs=[
            pl.BlockSpec((CONV_HALO, chans), lambda bi, i: (jnp.maximum((bi * nt + i) * per - 1, 0), 0)),
            pl.BlockSpec((tt, chans), lambda bi, i: (bi * nt + i, 0)),
            pl.BlockSpec((width, chans), fixed), pl.BlockSpec((1, chans), fixed),
            pl.BlockSpec((1, chans), fixed), pl.BlockSpec((1, chans), fixed),
        ],
        out_specs=pl.BlockSpec((tt, chans), lambda bi, i: (bi * nt + i, 0)),
        out_shape=jax.ShapeDtypeStruct((batch * seq, chans), BF16),
        scratch_shapes=[pltpu.VMEM((tt + CONV_HALO, chans), F32), pltpu.VMEM((tt, chans), F32)],
        compiler_params=_params(("parallel", "arbitrary"), blocks),
        name="conv_prompt",
    )(u, u, w_dw, b_dw, g, b)


def _conv_sample_kernel(prev_ref, u_ref, w_ref, bdw_ref, g_ref, b_ref, o_ref):
    taps = prev_ref.shape[0]
    c = u_ref[...] * w_ref[taps:taps + 1, :] + bdw_ref[...]
    for k in range(taps):
        c = c + prev_ref[k] * w_ref[k:k + 1, :]
    y = _layer_norm(c, g_ref[...], b_ref[...])
    o_ref[...] = (y * _sigmoid(y)).astype(o_ref.dtype)


def _conv_sample(prev_t, u, row_block, w_dw, b_dw, g, b):
    taps, nb, chans = prev_t.shape
    tb = _divisor_tile(nb, 16, 16)
    per = nb // tb
    blocks = 2 * (_nbytes((taps, tb, chans), F32) + _nbytes((taps + 1, chans), F32) + 2 * _nbytes((tb, chans), F32)) + 3 * _nbytes((tb, chans), F32)
    fixed = lambda i: (0, 0)
    return pl.pallas_call(
        _conv_sample_kernel,
        grid=(per,),
        in_specs=[
            pl.BlockSpec((taps, tb, chans), lambda i: (0, i, 0)),
            pl.BlockSpec((tb, chans), lambda i: (row_block * per + i, 0)),
            pl.BlockSpec((taps + 1, chans), fixed), pl.BlockSpec((1, chans), fixed),
            pl.BlockSpec((1, chans), fixed), pl.BlockSpec((1, chans), fixed),
        ],
        out_specs=pl.BlockSpec((tb, chans), lambda i: (i, 0)),
        out_shape=jax.ShapeDtypeStruct((nb, chans), BF16),
        compiler_params=_params(("parallel",), blocks),
        name="conv_sample",
    )(prev_t, u, w_dw, b_dw, g, b)


def _moba_prompt_kernel(q_ref, k_ref, v_ref, e_ref, nidx_ref, o_ref, kb_ref, vb_ref, mean_ref, *, group, hd, blk, nbk, topk, scale):
    i = pl.program_id(2)
    seq = k_ref.shape[0]

    @pl.when(i == 0)
    def _():
        k = k_ref[...]
        kb_ref[...] = k.astype(BF16)
        vb_ref[...] = v_ref[...].astype(BF16)
        mean_ref[...] = jnp.zeros_like(mean_ref)
        for n in range(nbk):
            mean_n = jnp.sum(k[n * blk:(n + 1) * blk], axis=0, keepdims=True) / blk
            for g in range(group):
                mean_ref[g * nbk + n:g * nbk + n + 1, g * hd:(g + 1) * hd] = mean_n

    q = q_ref[...]
    gate = lax.dot_general(q, mean_ref[...], NT_DIMS, precision=lax.Precision.HIGHEST, preferred_element_type=F32)
    nidx = nidx_ref[...]
    past = nidx < i
    gm = jnp.where(past, gate, -jnp.inf)
    rank = jnp.zeros(gm.shape, F32)
    for r in range(1, nbk):
        same_side = nidx >= r
        other = jnp.where(same_side, pltpu.roll(gm, r, 1), pltpu.roll(gm, (r - nbk) % LANES, 1))
        beats = jnp.logical_or(other > gm, jnp.logical_and(other == gm, same_side))
        rank = rank + jnp.where(beats, 1.0, 0.0)
    allowed = jnp.logical_or(jnp.logical_and(past, rank < topk), nidx == i)
    sel_bias = jnp.where(allowed, 0.0, MASKED).astype(BF16)

    rows = lax.broadcasted_iota(jnp.int32, (blk, seq), 0) + i * blk
    cols = lax.broadcasted_iota(jnp.int32, (blk, seq), 1)
    causal = cols <= rows
    qb = q.astype(BF16)
    kb = kb_ref[...]
    vb = vb_ref[...]
    for g in range(group):
        bias = jnp.dot(sel_bias, e_ref[g], preferred_element_type=F32)
        s = lax.dot_general(qb[:, g * hd:(g + 1) * hd], kb, NT_DIMS, preferred_element_type=F32) * scale + bias
        s = jnp.where(causal, s, MASKED)
        m = jnp.max(s, axis=-1, keepdims=True)
        p = jnp.exp(s - m)
        l = jnp.sum(p, axis=-1, keepdims=True)
        o = jnp.dot(p.astype(BF16), vb, preferred_element_type=F32) / l
        o_ref[:, g * hd:(g + 1) * hd] = o.astype(o_ref.dtype)


def _moba_prompt(qkv, batch, seq, n_q, n_kv, hd):
    blk = MOBA_BLOCK
    assert seq % blk == 0 and hd == LANES
    nbk = seq // blk
    group = n_q // n_kv
    assert group * nbk <= LANES
    topk = min(MOBA_TOPK, nbk)
    lane = np.arange(LANES)
    nidx = np.where(lane < group * nbk, lane % nbk, 1 << 20).astype(np.int32)[None, :]
    spread = np.zeros((group, LANES, seq), np.float32)
    for g in range(group):
        for n in range(nbk):
            spread[g, g * nbk + n, n * blk:(n + 1) * blk] = 1.0
    nqb = seq // blk
    blocks = (2 * (_nbytes((blk, group * hd), F32) + 2 * _nbytes((seq, hd), F32) + _nbytes((group, LANES, seq), BF16)
                   + _nbytes((blk, group * hd), BF16)) + 2 * _nbytes((seq, hd), BF16) + _nbytes((LANES, group * hd), F32)
              + 6 * _nbytes((blk, seq), F32))
    kern = functools.partial(_moba_prompt_kernel, group=group, hd=hd, blk=blk, nbk=nbk, topk=topk, scale=hd ** -0.5)
    return pl.pallas_call(
        kern,
        grid=(batch, n_kv, nqb),
        in_specs=[
            pl.BlockSpec((blk, group * hd), lambda b, h, i: (b * nqb + i, h)),
            pl.BlockSpec((seq, hd), lambda b, h, i: (b, n_q + h)),
            pl.BlockSpec((seq, hd), lambda b, h, i: (b, n_q + n_kv + h)),
            pl.BlockSpec((group, LANES, seq), lambda b, h, i: (0, 0, 0)),
            pl.BlockSpec((1, LANES), lambda b, h, i: (0, 0)),
        ],
        out_specs=pl.BlockSpec((blk, group * hd), lambda b, h, i: (b * nqb + i, h)),
        out_shape=jax.ShapeDtypeStruct((batch * seq, n_q * hd), BF16),
        scratch_shapes=[pltpu.VMEM((seq, hd), BF16), pltpu.VMEM((seq, hd), BF16), pltpu.VMEM((LANES, group * hd), F32)],
        compiler_params=_params(("parallel", "parallel", "arbitrary"), blocks),
        name="moba_prompt",
    )(qkv, qkv, qkv, jnp.asarray(spread, BF16), jnp.asarray(nidx))


def _moba_sample_kernel(pt_ref, q_ref, knew_ref, vnew_ref, mask_ref, *refs, pages_per_step, n_pages, group, n_kv, hd, page, topk, scale):
    del pt_ref
    k_refs = refs[:pages_per_step]
    v_refs = refs[pages_per_step:2 * pages_per_step]
    o_ref, m_ref, l_ref, acc_ref, ksum_ref = refs[2 * pages_per_step:]
    j = pl.program_id(1)
    q = q_ref[0]
    qb = q.astype(BF16)
    mask = mask_ref[...]
    n_q = q.shape[0]

    for p in range(pages_per_step):
        pg = j * pages_per_step + p
        kp = k_refs[p][0]
        s = lax.dot_general(qb, kp.astype(BF16), NT_DIMS, preferred_element_type=F32) * scale + mask
        m = jnp.max(s, axis=-1, keepdims=True)
        e = jnp.exp(s - m)
        m_ref[pg] = jnp.broadcast_to(m, (n_q, hd))
        l_ref[pg] = jnp.broadcast_to(jnp.sum(e, axis=-1, keepdims=True), (n_q, hd))
        acc_ref[pg] = jnp.dot(e.astype(BF16), v_refs[p][0].astype(BF16), preferred_element_type=F32)
        ksum_ref[pg] = jnp.sum(kp.reshape(page * n_kv // SUBLANES, SUBLANES, hd), axis=0)

    @pl.when(j == pl.num_programs(1) - 1)
    def _():
        ppb = MOBA_BLOCK // page
        nb = n_pages // ppb
        gates = []
        for n in range(nb):
            bs = ksum_ref[n * ppb]
            for t in range(1, ppb):
                bs = bs + ksum_ref[n * ppb + t]
            per_head = bs[0:n_kv]
            for t in range(1, SUBLANES // n_kv):
                per_head = per_head + bs[t * n_kv:(t + 1) * n_kv]
            mean = jnp.concatenate([per_head / MOBA_BLOCK] * group, axis=0)
            gates.append(jnp.sum(q * mean, axis=-1, keepdims=True))
        sel = []
        for n in range(nb):
            rank = jnp.zeros((n_q, 1), F32)
            for o in range(nb):
                if o != n:
                    beats = gates[o] >= gates[n] if o < n else gates[o] > gates[n]
                    rank = rank + jnp.where(beats, 1.0, 0.0)
            sel.append(rank < topk)
        k_new = jnp.concatenate([knew_ref[0]] * group, axis=0)
        v_new = jnp.concatenate([vnew_ref[0]] * group, axis=0)
        s_new = jnp.sum(q * k_new, axis=-1, keepdims=True) * scale
        m_tot = s_new
        for pg in range(n_pages):
            m_tot = jnp.maximum(m_tot, jnp.where(sel[pg // ppb], m_ref[pg][:, 0:1], MASKED))
        w_new = jnp.exp(s_new - m_tot)
        l_tot = w_new
        o_tot = w_new * v_new
        for pg in range(n_pages):
            w = jnp.where(sel[pg // ppb], jnp.exp(m_ref[pg][:, 0:1] - m_tot), 0.0)
            l_tot = l_tot + w * l_ref[pg][:, 0:1]
            o_tot = o_tot + w * acc_ref[pg]
        o_ref[0] = o_tot / l_tot


def _moba_sample(q, k_new, v_new, cache_k, cache_v, page_table_flat, layer, n_pool, n_pages, page, n_kv):
    nb, n_q, hd = q.shape
    group = n_q // n_kv
    assert MOBA_BLOCK % page == 0 and n_pages % (MOBA_BLOCK // page) == 0 and SUBLANES % n_kv == 0
    pps = _divisor_tile(n_pages, 8, 1)
    topk = min(MOBA_TOPK, n_pages // (MOBA_BLOCK // page) + 1)
    rows = np.arange(n_q)[:, None] % n_kv
    cols = np.arange(page * n_kv)[None, :] % n_kv
    mask = jnp.asarray(np.where(rows == cols, 0.0, MASKED).astype(np.float32))
    base = layer * n_pool

    def page_map(p):
        return lambda b, j, pt: (base + pt[b * n_pages + j * pps + p], 0, 0)

    per_seq = lambda b, j, pt: (b, 0, 0)
    cache_spec = [pl.BlockSpec((1, page * n_kv, hd), page_map(p)) for p in range(pps)]
    blocks = (4 * pps * _nbytes((page * n_kv, hd), F32) + 3 * n_pages * _nbytes((n_q, hd), F32)
              + n_pages * _nbytes((SUBLANES, hd), F32) + 8 * _nbytes((n_q, page * n_kv), F32))
    kern = functools.partial(_moba_sample_kernel, pages_per_step=pps, n_pages=n_pages, group=group, n_kv=n_kv, hd=hd,
                             page=page, topk=topk, scale=hd ** -0.5)
    return pl.pallas_call(
        kern,
        grid_spec=pltpu.PrefetchScalarGridSpec(
            num_scalar_prefetch=1,
            grid=(nb, n_pages // pps),
            in_specs=[
                pl.BlockSpec((1, n_q, hd), per_seq), pl.BlockSpec((1, n_kv, hd), per_seq), pl.BlockSpec((1, n_kv, hd), per_seq),
                pl.BlockSpec((n_q, page * n_kv), lambda b, j, pt: (0, 0)),
            ] + cache_spec + cache_spec,
            out_specs=pl.BlockSpec((1, n_q, hd), per_seq),
            scratch_shapes=[pltpu.VMEM((n_pages, n_q, hd), F32), pltpu.VMEM((n_pages, n_q, hd), F32),
                            pltpu.VMEM((n_pages, n_q, hd), F32), pltpu.VMEM((n_pages, SUBLANES, hd), F32)],
        ),
        out_shape=jax.ShapeDtypeStruct((nb, n_q, hd), F32),
        compiler_params=_params(("parallel", "arbitrary"), blocks),
        name="moba_sample",
    )(page_table_flat, q, k_new, v_new, mask, *([cache_k] * pps), *([cache_v] * pps))


def _mla_in_kernel(x_ref, w_ref, qg_ref, kvg_ref, cos_ref, sin_ref, cq_ref, ckv_ref, ckvb_ref, kr_ref, krb_ref, *, q_lora, kv_lora, half):
    h = jnp.dot(x_ref[...], w_ref[...], preferred_element_type=F32)

    def rms(x, g):
        return x * lax.rsqrt(jnp.mean(x * x, axis=-1, keepdims=True) + RMS_EPS) * g

    cq_ref[...] = rms(h[:, :q_lora], qg_ref[...]).astype(BF16)
    ckv = rms(h[:, q_lora:q_lora + kv_lora], kvg_ref[...])
    ckv_ref[...] = ckv
    ckvb_ref[...] = ckv.astype(BF16)
    kr = h[:, q_lora + kv_lora:]
    kr = kr * cos_ref[...] + _rotate_half(kr, half) * sin_ref[...]
    kr_ref[...] = kr
    krb_ref[...] = kr.astype(BF16)


def _mla_in(x, w, q_norm, kv_norm, cos, sin_signed, q_lora, kv_lora, half):
    n, d = x.shape
    wid = w.shape[1]
    tm = _divisor_tile(n, 640, 16)
    blocks = (2 * (_nbytes((tm, d), BF16) + _nbytes((d, wid), BF16) + 2 * _nbytes((tm, LANES), F32)
                   + _nbytes((tm, q_lora), BF16) + _nbytes((tm, kv_lora), F32) + _nbytes((tm, kv_lora), BF16)
                   + _nbytes((tm, LANES), F32) + _nbytes((tm, LANES), BF16)) + 3 * _nbytes((tm, wid), F32))
    row = lambda i: (i, 0)
    fixed = lambda i: (0, 0)
    return pl.pallas_call(
        functools.partial(_mla_in_kernel, q_lora=q_lora, kv_lora=kv_lora, half=half),
        grid=(n // tm,),
        in_specs=[
            pl.BlockSpec((tm, d), row), pl.BlockSpec((d, wid), fixed), pl.BlockSpec((1, q_lora), fixed),
            pl.BlockSpec((1, kv_lora), fixed), pl.BlockSpec((tm, LANES), row), pl.BlockSpec((tm, LANES), row),
        ],
        out_specs=[pl.BlockSpec((tm, q_lora), row), pl.BlockSpec((tm, kv_lora), row), pl.BlockSpec((tm, kv_lora), row),
                   pl.BlockSpec((tm, LANES), row), pl.BlockSpec((tm, LANES), row)],
        out_shape=[jax.ShapeDtypeStruct((n, q_lora), BF16), jax.ShapeDtypeStruct((n, kv_lora), F32),
                   jax.ShapeDtypeStruct((n, kv_lora), BF16), jax.ShapeDtypeStruct((n, LANES), F32),
                   jax.ShapeDtypeStruct((n, LANES), BF16)],
        compiler_params=_params(("parallel",), blocks),
        name="mla_in",
    )(x, w, q_norm, kv_norm, cos, sin_signed)


def _mla_prompt_kernel(qn_ref, qr_ref, kn_ref, kr_ref, v_ref, o_ref, *, scale):
    i = pl.program_id(2)
    tq = qn_ref.shape[0]
    seq = kn_ref.shape[0]
    s = (lax.dot_general(qn_ref[...], kn_ref[...], NT_DIMS, preferred_element_type=F32)
         + lax.dot_general(qr_ref[...], kr_ref[...], NT_DIMS, preferred_element_type=F32)) * scale
    rows = lax.broadcasted_iota(jnp.int32, (tq, seq), 0) + i * tq
    cols = lax.broadcasted_iota(jnp.int32, (tq, seq), 1)
    s = jnp.where(cols <= rows, s, MASKED)
    m = jnp.max(s, axis=-1, keepdims=True)
    p = jnp.exp(s - m)
    l = jnp.sum(p, axis=-1, keepdims=True)
    o_ref[...] = (jnp.dot(p.astype(BF16), v_ref[...], preferred_element_type=F32) / l).astype(o_ref.dtype)


def _mla_prompt(q2, kv, krb, batch, seq, heads, scale):
    tq = _divisor_tile(seq, 256, 16)
    nqb = seq // tq
    blocks = 2 * (3 * _nbytes((tq, LANES), BF16) + 3 * _nbytes((seq, LANES), BF16)) + 6 * _nbytes((tq, seq), F32)
    return pl.pallas_call(
        functools.partial(_mla_prompt_kernel, scale=scale),
        grid=(batch, heads, nqb),
        in_specs=[
            pl.BlockSpec((tq, LANES), lambda b, h, i: (b * nqb + i, h)),
            pl.BlockSpec((tq, LANES), lambda b, h, i: (b * nqb + i, heads + h)),
            pl.BlockSpec((seq, LANES), lambda b, h, i: (b, h)),
            pl.BlockSpec((seq, LANES), lambda b, h, i: (b, 0)),
            pl.BlockSpec((seq, LANES), lambda b, h, i: (b, heads + h)),
        ],
        out_specs=pl.BlockSpec((tq, LANES), lambda b, h, i: (b * nqb + i, h)),
        out_shape=jax.ShapeDtypeStruct((batch * seq, heads * LANES), BF16),
        compiler_params=_params(("parallel", "parallel", "parallel"), blocks),
        name="mla_prompt",
    )(q2, q2, kv, krb, kv)


def _head_mm_nt_kernel(x_ref, w_ref, o_ref):
    o_ref[...] = lax.dot_general(x_ref[...].astype(BF16), w_ref[...].astype(BF16), NT_DIMS, preferred_element_type=F32)


def _head_mm_kernel(x_ref, w_ref, o_ref):
    o_ref[...] = jnp.dot(x_ref[...].astype(BF16), w_ref[...].astype(BF16), preferred_element_type=F32).astype(o_ref.dtype)


def _mla_q_latent(q2, w_uk2d, row_block, nb, heads, kv_lora):
    blocks = 2 * (_nbytes((nb, LANES), BF16) + _nbytes((kv_lora, LANES), F32) + _nbytes((nb, kv_lora), F32)) * 2
    return pl.pallas_call(
        _head_mm_nt_kernel,
        grid=(heads,),
        in_specs=[pl.BlockSpec((nb, LANES), lambda h: (row_block, h)), pl.BlockSpec((kv_lora, LANES), lambda h: (0, h))],
        out_specs=pl.BlockSpec((nb, kv_lora), lambda h: (0, h)),
        out_shape=jax.ShapeDtypeStruct((nb, heads * kv_lora), F32),
        compiler_params=_params(("parallel",), blocks),
        name="mla_q_latent",
    )(q2, w_uk2d)


def _mla_out_latent(o_lat, w_uv2d, heads, kv_lora):
    nb = o_lat.shape[0]
    blocks = 2 * (_nbytes((nb, kv_lora), F32) + _nbytes((kv_lora, LANES), F32) + _nbytes((nb, LANES), F32)) * 2
    return pl.pallas_call(
        _head_mm_kernel,
        grid=(heads,),
        in_specs=[pl.BlockSpec((nb, kv_lora), lambda h: (0, h)), pl.BlockSpec((kv_lora, LANES), lambda h: (0, h))],
        out_specs=pl.BlockSpec((nb, LANES), lambda h: (0, h)),
        out_shape=jax.ShapeDtypeStruct((nb, heads * LANES), BF16),
        compiler_params=_params(("parallel",), blocks),
        name="mla_out_latent",
    )(o_lat, w_uv2d)


def _mla_sample_kernel(pt_ref, qlat_ref, qr_ref, cnew_ref, rnew_ref, *refs, pages_per_step, rope, scale):
    del pt_ref
    c_refs = refs[:pages_per_step]
    r_refs = refs[pages_per_step:2 * pages_per_step]
    o_ref, m_ref, l_ref, acc_ref = refs[2 * pages_per_step:]
    j = pl.program_id(1)
    qlat = qlat_ref[0]
    qr = qr_ref[0][:, :rope]
    heads = qlat.shape[0]

    @pl.when(j == 0)
    def _():
        c_new = cnew_ref[0]
        s_new = (jnp.sum(qlat * c_new, axis=-1, keepdims=True)
                 + jnp.sum(qr.astype(F32) * rnew_ref[0], axis=-1, keepdims=True)) * scale
        m_ref[...] = jnp.broadcast_to(s_new, m_ref.shape)
        l_ref[...] = jnp.ones_like(l_ref)
        acc_ref[...] = jnp.broadcast_to(c_new, acc_ref.shape)

    qlb = qlat.astype(BF16)
    qrb = qr.astype(BF16)
    m_run = m_ref[...][:, 0:1]
    l_run = l_ref[...][:, 0:1]
    acc = acc_ref[...]
    for p in range(pages_per_step):
        cb = c_refs[p][0].astype(BF16)
        rb = r_refs[p][0].astype(BF16)
        s = (lax.dot_general(qlb, cb, NT_DIMS, preferred_element_type=F32)
             + lax.dot_general(qrb, rb, NT_DIMS, preferred_element_type=F32)) * scale
        m_new = jnp.maximum(m_run, jnp.max(s, axis=-1, keepdims=True))
        corr = jnp.exp(m_run - m_new)
        e = jnp.exp(s - m_new)
        l_run = l_run * corr + jnp.sum(e, axis=-1, keepdims=True)
        acc = acc * corr + jnp.dot(e.astype(BF16), cb, preferred_element_type=F32)
        m_run = m_new
    m_ref[...] = jnp.broadcast_to(m_run, m_ref.shape)
    l_ref[...] = jnp.broadcast_to(l_run, l_ref.shape)
    acc_ref[...] = acc

    @pl.when(j == pl.num_programs(1) - 1)
    def _():
        o_ref[0] = acc / l_run


def _mla_sample(qlat, qr, c_new, r_new, lat_cache, rope_cache, page_table_flat, layer, n_pool, n_pages, scale):
    nb, heads, kv_lora = qlat.shape
    page = lat_cache.shape[1]
    rope = rope_cache.shape[2]
    pps = _divisor_tile(n_pages, 8, 1)
    base = layer * n_pool

    def page_map(p):
        return lambda b, j, pt: (base + pt[b * n_pages + j * pps + p], 0, 0)

    per_seq = lambda b, j, pt: (b, 0, 0)
    blocks = (2 * pps * (_nbytes((page, kv_lora), F32) + _nbytes((page, LANES), F32)) + 4 * _nbytes((heads, kv_lora), F32)
              + pps * _nbytes((page, kv_lora), F32))
    kern = functools.partial(_mla_sample_kernel, pages_per_step=pps, rope=rope, scale=scale)
    return pl.pallas_call(
        kern,
        grid_spec=pltpu.PrefetchScalarGridSpec(
            num_scalar_prefetch=1,
            grid=(nb, n_pages // pps),
            in_specs=[
                pl.BlockSpec((1, heads, kv_lora), per_seq), pl.BlockSpec((1, heads, LANES), per_seq),
                pl.BlockSpec((1, 1, kv_lora), per_seq), pl.BlockSpec((1, 1, rope), per_seq),
            ] + [pl.BlockSpec((1, page, kv_lora), page_map(p)) for p in range(pps)]
              + [pl.BlockSpec((1, page, rope), page_map(p)) for p in range(pps)],
            out_specs=pl.BlockSpec((1, heads, kv_lora), per_seq),
            scratch_shapes=[pltpu.VMEM((heads, LANES), F32), pltpu.VMEM((heads, LANES), F32), pltpu.VMEM((heads, kv_lora), F32)],
        ),
        out_shape=jax.ShapeDtypeStruct((nb, heads, kv_lora), F32),
        compiler_params=_params(("parallel", "arbitrary"), blocks),
        name="mla_sample",
    )(page_table_flat, qlat, qr, c_new, r_new, *([lat_cache] * pps), *([rope_cache] * pps))


def _rope_tables(pos, dim):
    inv = ROPE_THETA ** (-jnp.arange(0, dim, 2, dtype=F32) / dim)
    ang = pos.astype(F32)[:, None] * inv[None, :]
    ang = jnp.concatenate([ang, ang], axis=-1)
    sign = jnp.where(jnp.arange(dim) < dim // 2, -1.0, 1.0).astype(F32)
    cos, sin = jnp.cos(ang), jnp.sin(ang) * sign[None, :]
    if dim < LANES:
        pad = ((0, 0), (0, LANES - dim))
        cos, sin = jnp.pad(cos, pad), jnp.pad(sin, pad)
    return cos, sin


def kernel(x_prompt, x_sample, state_conv, cache_moba_k, cache_moba_v, cache_mla_latent, cache_mla_krope, page_table, conv_w_in, conv_b_in, conv_w_dw, conv_b_dw, conv_ln_g, conv_ln_b, conv_w_out, conv_b_out, moba_w_qkv, moba_w_o, mla_w_in, mla_q_norm, mla_w_uq, mla_kv_norm, mla_w_uk, mla_w_uv, mla_w_o, mlp_w1, mlp_w2, ln_g, ln_b):
    bp, sp, d = x_prompt.shape
    bs, ts, _ = x_sample.shape
    assert ts == 1 and (bp * sp) % bs == 0
    n_prompt = bp * sp
    depth = mlp_w1.shape[0]
    alpha = (2 * depth) ** 0.25
    n_pages = page_table.shape[1]
    n_pool, page, n_kv, hd = cache_moba_k.shape[1:]
    past_len = n_pages * page
    assert past_len % MOBA_BLOCK == 0
    n_q = moba_w_qkv.shape[2] // hd - 2 * n_kv
    q_lora = mla_q_norm.shape[1]
    kv_lora, heads, nope = mla_w_uk.shape[1:]
    v_dim = mla_w_uv.shape[3]
    rope = cache_mla_krope.shape[3]
    conv_width = conv_w_dw.shape[1]
    assert nope == LANES and v_dim == LANES and rope <= LANES and sp >= conv_width - 1
    mla_scale = (nope + rope) ** -0.5

    pos = jnp.concatenate([jnp.tile(jnp.arange(sp, dtype=jnp.int32), bp), jnp.full((bs,), past_len, jnp.int32)])
    cos_moba, sin_moba = _rope_tables(pos, hd)
    cos_mla, sin_mla = _rope_tables(pos, rope)
    page_table_flat = page_table.reshape(-1).astype(jnp.int32)
    row1 = lambda v: v.reshape(1, -1)

    h = jnp.concatenate([x_prompt.reshape(n_prompt, d), x_sample.reshape(bs, d)], axis=0)
    hb = h.astype(BF16)

    conv_p, conv_s = [], []
    mk_p, mv_p, mk_s, mv_s = [], [], [], []
    ml_p, mr_p, ml_s, mr_s = [], [], [], []
    for i in range(depth):
        kind, j = i % 3, i // 3
        if kind == 0:
            u = _glu_matmul(hb, conv_w_in[j].astype(BF16), row1(conv_b_in[j]))
            mixed_p = _conv_prompt(u, conv_w_dw[j], row1(conv_b_dw[j]), row1(conv_ln_g[j]), row1(conv_ln_b[j]), bp, sp)
            prev = state_conv[j].astype(F32)
            mixed_s = _conv_sample(prev.transpose(1, 0, 2), u, n_prompt // bs, conv_w_dw[j], row1(conv_b_dw[j]),
                                   row1(conv_ln_g[j]), row1(conv_ln_b[j]))
            conv_p.append(u[:n_prompt].reshape(bp, sp, -1)[:, sp - (conv_width - 1):])
            conv_s.append(jnp.concatenate([prev[:, 1:], u[n_prompt:].reshape(bs, 1, -1)], axis=1))
            mixed = jnp.concatenate([mixed_p, mixed_s], axis=0)
            w_out, b_out = conv_w_out[j].astype(BF16), row1(conv_b_out[j])
        elif kind == 1:
            nq_cols, nk_cols = n_q * hd, n_kv * hd
            qkv = _matmul_rope(hb, moba_w_qkv[j].astype(BF16), cos_moba, sin_moba, F32, (0, nq_cols + nk_cols), hd // 2,
                               tn=math.gcd(nq_cols, nk_cols))
            mixed_p = _moba_prompt(qkv, bp, sp, n_q, n_kv, hd)
            group = n_q // n_kv
            q_s = qkv[n_prompt:, :nq_cols].reshape(bs, n_kv, group, hd).transpose(0, 2, 1, 3).reshape(bs, n_q, hd)
            k_s = qkv[n_prompt:, nq_cols:nq_cols + nk_cols].reshape(bs, n_kv, hd)
            v_s = qkv[n_prompt:, nq_cols + nk_cols:].reshape(bs, n_kv, hd)
            o_s = _moba_sample(q_s, k_s, v_s,
                               cache_moba_k.reshape(-1, page * n_kv, hd), cache_moba_v.reshape(-1, page * n_kv, hd),
                               page_table_flat, j, n_pool, n_pages, page, n_kv)
            mixed_s = o_s.reshape(bs, group, n_kv, hd).transpose(0, 2, 1, 3).reshape(bs, nq_cols).astype(BF16)
            mk_p.append(qkv[:n_prompt, nq_cols:nq_cols + nk_cols].reshape(bp, sp, n_kv, hd))
            mv_p.append(qkv[:n_prompt, nq_cols + nk_cols:].reshape(bp, sp, n_kv, hd))
            mk_s.append(k_s.reshape(bs, 1, n_kv, hd))
            mv_s.append(v_s.reshape(bs, 1, n_kv, hd))
            mixed = jnp.concatenate([mixed_p, mixed_s], axis=0)
            w_out, b_out = moba_w_o[j].astype(BF16), jnp.zeros((1, d), F32)
        else:
            w_in = jnp.pad(mla_w_in[j], ((0, 0), (0, LANES - rope))).astype(BF16)
            cq, ckv, ckv_b, kr, kr_b = _mla_in(hb, w_in, row1(mla_q_norm[j]), row1(mla_kv_norm[j]), cos_mla, sin_mla,
                                               q_lora, kv_lora, rope // 2)
            w_uq = mla_w_uq[j].reshape(q_lora, heads, nope + rope)
            w_uq_rope = jnp.pad(w_uq[:, :, nope:], ((0, 0), (0, 0), (0, LANES - rope)))
            w_uq2 = jnp.concatenate([w_uq[:, :, :nope].reshape(q_lora, heads * nope),
                                     w_uq_rope.reshape(q_lora, heads * LANES)], axis=1).astype(BF16)
            q2 = _matmul_rope(cq, w_uq2, cos_mla, sin_mla, BF16, (heads * nope, heads * (nope + LANES)), rope // 2,
                              tn=_divisor_tile(heads * LANES, 512, LANES))
            w_uk2d = mla_w_uk[j].reshape(kv_lora, heads * nope)
            w_uv2d = mla_w_uv[j].reshape(kv_lora, heads * v_dim)
            kv = _matmul(ckv_b, jnp.concatenate([w_uk2d, w_uv2d], axis=1).astype(BF16), BF16, rows=n_prompt)
            mixed_p = _mla_prompt(q2, kv, kr_b, bp, sp, heads, mla_scale)
            qlat = _mla_q_latent(q2, w_uk2d, n_prompt // bs, bs, heads, kv_lora).reshape(bs, heads, kv_lora)
            qr_s = q2[n_prompt:, heads * nope:].reshape(bs, heads, LANES)
            c_new = ckv[n_prompt:].reshape(bs, 1, kv_lora)
            r_new = kr[n_prompt:, :rope].reshape(bs, 1, rope)
            o_lat = _mla_sample(qlat, qr_s, c_new, r_new,
                                cache_mla_latent.reshape(-1, page, kv_lora), cache_mla_krope.reshape(-1, page, rope),
                                page_table_flat, j, cache_mla_latent.shape[1], n_pages, mla_scale)
            mixed_s = _mla_out_latent(o_lat.reshape(bs, heads * kv_lora), w_uv2d, heads, kv_lora)
            ml_p.append(ckv[:n_prompt].reshape(bp, sp, kv_lora))
            mr_p.append(kr[:n_prompt, :rope].reshape(bp, sp, rope))
            ml_s.append(c_new)
            mr_s.append(r_new)
            mixed = jnp.concatenate([mixed_p, mixed_s], axis=0)
            w_out, b_out = mla_w_o[j].astype(BF16), jnp.zeros((1, d), F32)
        h, hb = _matmul_res_ln(mixed, w_out, b_out, h, row1(ln_g[i, 0]), row1(ln_b[i, 0]), alpha)
        h, hb = _mlp_res_ln(hb, h, mlp_w1[i].astype(BF16), mlp_w2[i].astype(BF16), row1(ln_g[i, 1]), row1(ln_b[i, 1]), alpha)

    return (h[:n_prompt].reshape(bp, sp, d), h[n_prompt:].reshape(bs, 1, d),
            jnp.stack(conv_p), jnp.stack(conv_s),
            jnp.stack(mk_p), jnp.stack(mv_p), jnp.stack(mk_s), jnp.stack(mv_s),
            jnp.stack(ml_p), jnp.stack(mr_p), jnp.stack(ml_s), jnp.stack(mr_s))
```

```python
import functools
import math

import numpy as np
import jax
import jax.numpy as jnp
from jax import lax
from jax.experimental import pallas as pl
from jax.experimental.pallas import tpu as pltpu

F32 = jnp.float32
BF16 = jnp.bfloat16

MOBA_BLOCK = 256
MOBA_TOPK = 3
ROPE_THETA = 10000.0
LN_EPS = 1e-5
RMS_EPS = 1e-6
MASKED = -1e30

LANES = 128
SUBLANES = 8
VMEM_CAP_BYTES = 60 * 1024 * 1024
VMEM_FLOOR_BYTES = 32 * 1024 * 1024

NT_DIMS = (((1,), (1,)), ((), ()))


def _divisor_tile(n, target, mult):
    best = None
    for t in range(mult, min(n, target) + 1, mult):
        if n % t == 0:
            best = t
    return n if best is None else best


def _params(semantics, block_bytes):
    limit = int(min(max(block_bytes * 1.3 + (6 << 20), VMEM_FLOOR_BYTES), VMEM_CAP_BYTES))
    return pltpu.CompilerParams(dimension_semantics=semantics, vmem_limit_bytes=limit)


def _nbytes(shape, dtype):
    return int(np.prod(shape)) * jnp.dtype(dtype).itemsize


def _layer_norm(z, g, b):
    mu = jnp.mean(z, axis=-1, keepdims=True)
    zc = z - mu
    var = jnp.mean(zc * zc, axis=-1, keepdims=True)
    return zc * lax.rsqrt(var + LN_EPS) * g + b


def _sigmoid(x):
    return 1.0 / (1.0 + jnp.exp(-x))


def _mm_kernel(x_ref, w_ref, o_ref):
    o_ref[...] = jnp.dot(x_ref[...], w_ref[...], preferred_element_type=F32).astype(o_ref.dtype)


def _matmul(x, w, out_dtype, rows=None, tm_target=1024, tn_target=512):
    rows = x.shape[0] if rows is None else rows
    k, nout = w.shape
    tm = _divisor_tile(rows, tm_target, 16)
    tn = _divisor_tile(nout, tn_target, LANES)
    blocks = 2 * (_nbytes((tm, k), BF16) + _nbytes((k, tn), BF16) + _nbytes((tm, tn), out_dtype)) + _nbytes((tm, tn), F32)
    return pl.pallas_call(
        _mm_kernel,
        grid=(rows // tm, nout // tn),
        in_specs=[pl.BlockSpec((tm, k), lambda i, j: (i, 0)), pl.BlockSpec((k, tn), lambda i, j: (0, j))],
        out_specs=pl.BlockSpec((tm, tn), lambda i, j: (i, j)),
        out_shape=jax.ShapeDtypeStruct((rows, nout), out_dtype),
        compiler_params=_params(("parallel", "parallel"), blocks),
        name="matmul",
    )(x, w)


def _glu_kernel(x_ref, wa_ref, wg_ref, ba_ref, bg_ref, o_ref):
    x = x_ref[...]
    a = jnp.dot(x, wa_ref[...], preferred_element_type=F32) + ba_ref[...]
    g = jnp.dot(x, wg_ref[...], preferred_element_type=F32) + bg_ref[...]
    o_ref[...] = a * _sigmoid(g)


def _glu_matmul(x, w, layer, b):
    n, k = x.shape
    c = w.shape[2] // 2
    tm = _divisor_tile(n, 1024, 16)
    tn = _divisor_tile(c, 512, LANES)
    nj = c // tn
    blocks = 2 * (_nbytes((tm, k), BF16) + 2 * _nbytes((k, tn), BF16) + _nbytes((tm, tn), F32)) + 3 * _nbytes((tm, tn), F32)
    return pl.pallas_call(
        _glu_kernel,
        grid=(n // tm, nj),
        in_specs=[
            pl.BlockSpec((tm, k), lambda i, j: (i, 0)),
            pl.BlockSpec((None, k, tn), lambda i, j: (layer, 0, j)),
            pl.BlockSpec((None, k, tn), lambda i, j: (layer, 0, nj + j)),
            pl.BlockSpec((1, tn), lambda i, j: (0, j)),
            pl.BlockSpec((1, tn), lambda i, j: (0, nj + j)),
        ],
        out_specs=pl.BlockSpec((tm, tn), lambda i, j: (i, j)),
        out_shape=jax.ShapeDtypeStruct((n, c), F32),
        compiler_params=_params(("parallel", "parallel"), blocks),
        name="glu_matmul",
    )(x, w, w, b, b)


def _rotate_half(a, half):
    if 2 * half == LANES:
        return pltpu.roll(a, half, 1)
    lane = lax.broadcasted_iota(jnp.int32, a.shape, 1)
    return jnp.where(lane % (2 * half) < half, pltpu.roll(a, LANES - half, 1), pltpu.roll(a, half, 1))


def _mm_rope_kernel(x_ref, w_ref, cos_ref, sin_ref, o_ref, *, rope_lo, rope_hi, half):
    acc = jnp.dot(x_ref[...], w_ref[...], preferred_element_type=F32)
    j = pl.program_id(1)
    in_rope = jnp.logical_and(j >= rope_lo, j < rope_hi)

    @pl.when(in_rope)
    def _():
        cos = cos_ref[...]
        sin = sin_ref[...]
        for c0 in range(0, acc.shape[1], LANES):
            a = acc[:, c0:c0 + LANES]
            o_ref[:, c0:c0 + LANES] = (a * cos + _rotate_half(a, half) * sin).astype(o_ref.dtype)

    @pl.when(jnp.logical_not(in_rope))
    def _():
        o_ref[...] = acc.astype(o_ref.dtype)


def _matmul_rope(x, w, cos, sin_signed, out_dtype, rope_cols, half, tn):
    n, k = x.shape
    nout = w.shape[1]
    tm = _divisor_tile(n, 1024, 16)
    assert nout % tn == 0 and rope_cols[0] % tn == 0 and rope_cols[1] % tn == 0
    blocks = 2 * (_nbytes((tm, k), BF16) + _nbytes((k, tn), BF16) + 2 * _nbytes((tm, LANES), F32)
                  + _nbytes((tm, tn), out_dtype)) + 2 * _nbytes((tm, tn), F32)
    kern = functools.partial(_mm_rope_kernel, rope_lo=rope_cols[0] // tn, rope_hi=rope_cols[1] // tn, half=half)
    return pl.pallas_call(
        kern,
        grid=(n // tm, nout // tn),
        in_specs=[
            pl.BlockSpec((tm, k), lambda i, j: (i, 0)),
            pl.BlockSpec((k, tn), lambda i, j: (0, j)),
            pl.BlockSpec((tm, LANES), lambda i, j: (i, 0)),
            pl.BlockSpec((tm, LANES), lambda i, j: (i, 0)),
        ],
        out_specs=pl.BlockSpec((tm, tn), lambda i, j: (i, j)),
        out_shape=jax.ShapeDtypeStruct((n, nout), out_dtype),
        compiler_params=_params(("parallel", "parallel"), blocks),
        name="matmul_rope",
    )(x, w, cos, sin_signed)


def _mm_res_ln_kernel(x_ref, w_ref, bias_ref, res_ref, g_ref, b_ref, of_ref, ob_ref, *, alpha):
    y = jnp.dot(x_ref[...], w_ref[...], preferred_element_type=F32) + bias_ref[...]
    out = _layer_norm(alpha * res_ref[...] + y, g_ref[...], b_ref[...])
    of_ref[...] = out
    ob_ref[...] = out.astype(BF16)


def _matmul_res_ln(x, w, layer, bias, res, g, b, alpha):
    n, k = x.shape
    d = w.shape[2]
    tm = _divisor_tile(n, 640, 16)
    blocks = (2 * (_nbytes((tm, k), BF16) + _nbytes((k, d), BF16) + 2 * _nbytes((tm, d), F32) + _nbytes((tm, d), BF16))
              + 3 * _nbytes((tm, d), F32))
    row = lambda i: (i, 0)
    fixed = lambda i: (0, 0)
    return pl.pallas_call(
        functools.partial(_mm_res_ln_kernel, alpha=alpha),
        grid=(n // tm,),
        in_specs=[
            pl.BlockSpec((tm, k), row), pl.BlockSpec((None, k, d), lambda i: (layer, 0, 0)), pl.BlockSpec((1, d), fixed),
            pl.BlockSpec((tm, d), row), pl.BlockSpec((1, d), fixed), pl.BlockSpec((1, d), fixed),
        ],
        out_specs=[pl.BlockSpec((tm, d), row), pl.BlockSpec((tm, d), row)],
        out_shape=[jax.ShapeDtypeStruct((n, d), F32), jax.ShapeDtypeStruct((n, d), BF16)],
        compiler_params=_params(("parallel",), blocks),
        name="matmul_res_ln",
    )(x, w, bias, res, g, b)


def _mlp_kernel(xb_ref, xf_ref, w1_ref, w2_ref, g_ref, b_ref, of_ref, ob_ref, acc_ref, *, alpha):
    f = pl.program_id(1)

    @pl.when(f == 0)
    def _():
        acc_ref[...] = jnp.zeros_like(acc_ref)

    h1 = jnp.maximum(jnp.dot(xb_ref[...], w1_ref[...], preferred_element_type=F32), 0.0)
    acc_ref[...] += jnp.dot((h1 * h1).astype(BF16), w2_ref[...], preferred_element_type=F32)

    @pl.when(f == pl.num_programs(1) - 1)
    def _():
        out = _layer_norm(alpha * xf_ref[...] + acc_ref[...], g_ref[...], b_ref[...])
        of_ref[...] = out
        ob_ref[...] = out.astype(BF16)


def _mlp_res_ln(xb, xf, w1, w2, layer, g, b, alpha):
    n, d = xb.shape
    ff = w1.shape[2]
    tm = _divisor_tile(n, 640, 16)
    tf = _divisor_tile(ff, 512, LANES)
    blocks = (2 * (_nbytes((tm, d), BF16) * 2 + 2 * _nbytes((tm, d), F32) + _nbytes((d, tf), BF16) + _nbytes((tf, d), BF16))
              + _nbytes((tm, d), F32) + 2 * _nbytes((tm, tf), F32))
    row = lambda i, f: (i, 0)
    fixed = lambda i, f: (0, 0)
    return pl.pallas_call(
        functools.partial(_mlp_kernel, alpha=alpha),
        grid=(n // tm, ff // tf),
        in_specs=[
            pl.BlockSpec((tm, d), row), pl.BlockSpec((tm, d), row),
            pl.BlockSpec((None, d, tf), lambda i, f: (layer, 0, f)), pl.BlockSpec((None, tf, d), lambda i, f: (layer, f, 0)),
            pl.BlockSpec((1, d), fixed), pl.BlockSpec((1, d), fixed),
        ],
        out_specs=[pl.BlockSpec((tm, d), row), pl.BlockSpec((tm, d), row)],
        out_shape=[jax.ShapeDtypeStruct((n, d), F32), jax.ShapeDtypeStruct((n, d), BF16)],
        scratch_shapes=[pltpu.VMEM((tm, d), F32)],
        compiler_params=_params(("parallel", "arbitrary"), blocks),
        name="mlp_res_ln",
    )(xb, xf, w1, w2, g, b)


CONV_HALO = 32
CONV_ROWS = 128
CONV_COLS = 128


def _conv_prompt_kernel(halo_ref, u_ref, w_ref, bdw_ref, g_ref, b_ref, o_ref, win_ref, sh_ref, c_ref, *, width, tt):
    first = pl.program_id(1) == 0
    win_ref[0:CONV_HALO, :] = jnp.where(first, 0.0, halo_ref[...])
    win_ref[CONV_HALO:CONV_HALO + tt, :] = u_ref[...]
    chans = u_ref.shape[1]
    off0 = CONV_HALO - (width - 1)
    nrows = sh_ref.shape[1]
    for s in range(1, SUBLANES):
        sh_ref[s - 1] = win_ref[s:s + nrows, :]
    def chunk(ci, carry):
        cs = pl.ds(pl.multiple_of(ci * CONV_COLS, CONV_COLS), CONV_COLS)
        for r0 in range(0, tt, CONV_ROWS):
            acc = jnp.zeros((CONV_ROWS, CONV_COLS), F32)
            for k in range(width):
                s = (off0 + k) % SUBLANES
                row = r0 + off0 + k - s
                if s:
                    x = sh_ref[s - 1, row:row + CONV_ROWS, cs]
                else:
                    x = win_ref[row:row + CONV_ROWS, cs]
                acc = acc + w_ref[k:k + 1, cs] * x
            c_ref[r0:r0 + CONV_ROWS, cs] = acc + bdw_ref[:, cs]
        return carry

    lax.fori_loop(0, chans // CONV_COLS, chunk, 0)
    y = _layer_norm(c_ref[...], g_ref[...], b_ref[...])
    o_ref[...] = (y * _sigmoid(y)).astype(o_ref.dtype)


def _conv_prompt(u, w_dw, b_dw, g, b, batch, seq):
    chans = u.shape[1]
    width = w_dw.shape[0]
    tt = _divisor_tile(seq, 128, CONV_ROWS)
    assert width - 1 <= CONV_HALO and tt % CONV_HALO == 0 and tt % CONV_ROWS == 0 and chans % CONV_COLS == 0
    nt = seq // tt
    per = tt // CONV_HALO
    shift_rows = tt + (CONV_HALO - SUBLANES)
    blocks = (2 * (_nbytes((CONV_HALO, chans), F32) + _nbytes((tt, chans), F32) + _nbytes((width, chans), F32)
                   + _nbytes((tt, chans), BF16)) + _nbytes((tt + CONV_HALO, chans), F32) + 3 * _nbytes((tt, chans), F32)
              + (SUBLANES - 1) * _nbytes((shift_rows, chans), F32))
    fixed = lambda bi, i: (0, 0)
    return pl.pallas_call(
        functools.partial(_conv_prompt_kernel, width=width, tt=tt),
        grid=(batch, nt),
        in_specs=[
            pl.BlockSpec((CONV_HALO, chans), lambda bi, i: (jnp.maximum((bi * nt + i) * per - 1, 0), 0)),
            pl.BlockSpec((tt, chans), lambda bi, i: (bi * nt + i, 0)),
            pl.BlockSpec((width, chans), fixed), pl.BlockSpec((1, chans), fixed),
            pl.BlockSpec((1, chans), fixed), pl.BlockSpec((1, chans), fixed),
        ],
        out_specs=pl.BlockSpec((tt, chans), lambda bi, i: (bi * nt + i, 0)),
        out_shape=jax.ShapeDtypeStruct((batch * seq, chans), BF16),
        scratch_shapes=[pltpu.VMEM((tt + CONV_HALO, chans), F32), pltpu.VMEM((SUBLANES - 1, shift_rows, chans), F32),
                        pltpu.VMEM((tt, chans), F32)],
        compiler_params=_params(("parallel", "arbitrary"), blocks),
        name="conv_prompt",
    )(u, u, w_dw, b_dw, g, b)


def _conv_sample_kernel(prev_ref, u_ref, w_ref, bdw_ref, g_ref, b_ref, o_ref):
    taps = prev_ref.shape[0]
    c = u_ref[...] * w_ref[taps:taps + 1, :] + bdw_ref[...]
    for k in range(taps):
        c = c + prev_ref[k] * w_ref[k:k + 1, :]
    y = _layer_norm(c, g_ref[...], b_ref[...])
    o_ref[...] = (y * _sigmoid(y)).astype(o_ref.dtype)


def _conv_sample(prev_t, u, row_block, w_dw, b_dw, g, b):
    taps, nb, chans = prev_t.shape
    tb = _divisor_tile(nb, 16, 16)
    per = nb // tb
    blocks = 2 * (_nbytes((taps, tb, chans), F32) + _nbytes((taps + 1, chans), F32) + 2 * _nbytes((tb, chans), F32)) + 3 * _nbytes((tb, chans), F32)
    fixed = lambda i: (0, 0)
    return pl.pallas_call(
        _conv_sample_kernel,
        grid=(per,),
        in_specs=[
            pl.BlockSpec((taps, tb, chans), lambda i: (0, i, 0)),
            pl.BlockSpec((tb, chans), lambda i: (row_block * per + i, 0)),
            pl.BlockSpec((taps + 1, chans), fixed), pl.BlockSpec((1, chans), fixed),
            pl.BlockSpec((1, chans), fixed), pl.BlockSpec((1, chans), fixed),
        ],
        out_specs=pl.BlockSpec((tb, chans), lambda i: (i, 0)),
        out_shape=jax.ShapeDtypeStruct((nb, chans), BF16),
        compiler_params=_params(("parallel",), blocks),
        name="conv_sample",
    )(prev_t, u, w_dw, b_dw, g, b)


def _moba_prompt_kernel(q_ref, k_ref, v_ref, e_ref, nidx_ref, o_ref, kb_ref, vb_ref, mean_ref, *, group, hd, blk, nbk, topk, scale):
    i = pl.program_id(2)

    @pl.when(i == 0)
    def _():
        k = k_ref[...]
        kb_ref[...] = k.astype(BF16)
        vb_ref[...] = v_ref[...].astype(BF16)
        mean_ref[...] = jnp.zeros_like(mean_ref)
        for n in range(nbk):
            mean_n = jnp.sum(k[n * blk:(n + 1) * blk], axis=0, keepdims=True) / blk
            for g in range(group):
                mean_ref[g * nbk + n:g * nbk + n + 1, g * hd:(g + 1) * hd] = mean_n

    q = q_ref[...]
    gate = lax.dot_general(q, mean_ref[...], NT_DIMS, precision=lax.Precision.HIGHEST, preferred_element_type=F32)
    nidx = nidx_ref[...]
    past = nidx < i
    gm = jnp.where(past, gate, -jnp.inf)
    rank = jnp.zeros(gm.shape, F32)
    for r in range(1, nbk):
        same_side = nidx >= r
        other = jnp.where(same_side, pltpu.roll(gm, r, 1), pltpu.roll(gm, (r - nbk) % LANES, 1))
        beats = jnp.logical_or(other > gm, jnp.logical_and(other == gm, same_side))
        rank = rank + jnp.where(beats, 1.0, 0.0)
    selected = jnp.logical_and(past, rank < topk)
    sel_bias = jnp.where(selected, 0.0, MASKED).astype(BF16)
    qb = q.astype(BF16)
    lower = lax.broadcasted_iota(jnp.int32, (blk, blk), 1) <= lax.broadcasted_iota(jnp.int32, (blk, blk), 0)

    def attend(own):
        past_keys = own * blk
        k_own = kb_ref[past_keys:past_keys + blk, :]
        v_own = vb_ref[past_keys:past_keys + blk, :]
        for g in range(group):
            qg = qb[:, g * hd:(g + 1) * hd]
            s_own = jnp.where(lower, lax.dot_general(qg, k_own, NT_DIMS, preferred_element_type=F32) * scale, MASKED)
            m = jnp.max(s_own, axis=-1, keepdims=True)
            if own:
                bias = jnp.dot(sel_bias, e_ref[g, :, 0:past_keys], preferred_element_type=F32)
                s_past = lax.dot_general(qg, kb_ref[0:past_keys, :], NT_DIMS, preferred_element_type=F32) * scale + bias
                m = jnp.maximum(m, jnp.max(s_past, axis=-1, keepdims=True))
                p_past = jnp.exp(s_past - m)
            p_own = jnp.exp(s_own - m)
            l = jnp.sum(p_own, axis=-1, keepdims=True)
            o = jnp.dot(p_own.astype(BF16), v_own, preferred_element_type=F32)
            if own:
                l = l + jnp.sum(p_past, axis=-1, keepdims=True)
                o = o + jnp.dot(p_past.astype(BF16), vb_ref[0:past_keys, :], preferred_element_type=F32)
            o_ref[:, g * hd:(g + 1) * hd] = (o / l).astype(o_ref.dtype)

    for own in range(nbk):
        pl.when(i == own)(functools.partial(attend, own))


def _moba_prompt(qkv, batch, seq, n_q, n_kv, hd):
    blk = MOBA_BLOCK
    assert seq % blk == 0 and hd == LANES
    nbk = seq // blk
    group = n_q // n_kv
    assert group * nbk <= LANES
    topk = min(MOBA_TOPK, nbk)
    lane = np.arange(LANES)
    nidx = np.where(lane < group * nbk, lane % nbk, 1 << 20).astype(np.int32)[None, :]
    spread = np.zeros((group, LANES, seq), np.float32)
    for g in range(group):
        for n in range(nbk):
            spread[g, g * nbk + n, n * blk:(n + 1) * blk] = 1.0
    nqb = seq // blk
    blocks = (2 * (_nbytes((blk, group * hd), F32) + 2 * _nbytes((seq, hd), F32) + _nbytes((group, LANES, seq), BF16)
                   + _nbytes((blk, group * hd), BF16)) + 2 * _nbytes((seq, hd), BF16) + _nbytes((LANES, group * hd), F32)
              + 6 * _nbytes((blk, seq), F32))
    kern = functools.partial(_moba_prompt_kernel, group=group, hd=hd, blk=blk, nbk=nbk, topk=topk, scale=hd ** -0.5)
    return pl.pallas_call(
        kern,
        grid=(batch, n_kv, nqb),
        in_specs=[
            pl.BlockSpec((blk, group * hd), lambda b, h, i: (b * nqb + i, h)),
            pl.BlockSpec((seq, hd), lambda b, h, i: (b, n_q + h)),
            pl.BlockSpec((seq, hd), lambda b, h, i: (b, n_q + n_kv + h)),
            pl.BlockSpec((group, LANES, seq), lambda b, h, i: (0, 0, 0)),
            pl.BlockSpec((1, LANES), lambda b, h, i: (0, 0)),
        ],
        out_specs=pl.BlockSpec((blk, group * hd), lambda b, h, i: (b * nqb + i, h)),
        out_shape=jax.ShapeDtypeStruct((batch * seq, n_q * hd), BF16),
        scratch_shapes=[pltpu.VMEM((seq, hd), BF16), pltpu.VMEM((seq, hd), BF16), pltpu.VMEM((LANES, group * hd), F32)],
        compiler_params=_params(("parallel", "parallel", "arbitrary"), blocks),
        name="moba_prompt",
    )(qkv, qkv, qkv, jnp.asarray(spread, BF16), jnp.asarray(nidx))


def _rowsum8(x):
    parts = x.reshape(x.shape[0] // SUBLANES, SUBLANES, x.shape[1])
    while parts.shape[0] > 1 and parts.shape[0] % 2 == 0:
        half = parts.shape[0] // 2
        parts = parts[:half] + parts[half:]
    return jnp.sum(parts, axis=0)


def _moba_sample_kernel(pt_ref, q_ref, knew_ref, vnew_ref, mask_ref, lane_blk_ref, lane_kv_ref, *refs, pages_per_step, ppb, nb, group, n_kv, hd, topk, scale):
    del pt_ref
    k_refs = refs[:pages_per_step]
    v_refs = refs[pages_per_step:2 * pages_per_step]
    o_ref, m_ref, l_ref, acc_ref, ksum_ref, mean_ref = refs[2 * pages_per_step:]
    j = pl.program_id(1)
    q = q_ref[0]
    qb = q.astype(BF16)
    mask = mask_ref[...]
    n_q = q.shape[0]
    lane_blk = lane_blk_ref[...]
    bps = pages_per_step // ppb

    @pl.when(j == 0)
    def _():
        m_ref[...] = jnp.zeros_like(m_ref)
        l_ref[...] = jnp.zeros_like(l_ref)

    kps = [k_refs[p][0] for p in range(pages_per_step)]
    scores = [lax.dot_general(qb, kp.astype(BF16), NT_DIMS, preferred_element_type=F32) for kp in kps]
    m_all = m_ref[...]
    l_all = l_ref[...]
    for bi in range(bps):
        blk = j * bps + bi
        s = jnp.concatenate(scores[bi * ppb:(bi + 1) * ppb], axis=-1) * scale + mask
        m = jnp.max(s, axis=-1, keepdims=True)
        e = jnp.exp(s - m)
        l = jnp.sum(e, axis=-1, keepdims=True)
        eb = e.astype(BF16)
        cols = s.shape[1] // ppb
        acc = jnp.dot(eb[:, 0:cols], v_refs[bi * ppb][0].astype(BF16), preferred_element_type=F32)
        ksum = _rowsum8(kps[bi * ppb])
        for t in range(1, ppb):
            acc = acc + jnp.dot(eb[:, t * cols:(t + 1) * cols], v_refs[bi * ppb + t][0].astype(BF16), preferred_element_type=F32)
            ksum = ksum + _rowsum8(kps[bi * ppb + t])
        hit = lane_blk == blk
        m_all = jnp.where(hit, m, m_all)
        l_all = jnp.where(hit, l, l_all)
        acc_ref[blk] = acc
        ksum_ref[blk] = ksum
    m_ref[...] = m_all
    l_ref[...] = l_all

    @pl.when(j == pl.num_programs(1) - 1)
    def _():
        mean_ref[...] = jnp.zeros_like(mean_ref)
        for n in range(nb):
            bs = ksum_ref[n]
            per_head = bs[0:n_kv]
            for t in range(1, SUBLANES // n_kv):
                per_head = per_head + bs[t * n_kv:(t + 1) * n_kv]
            mean_ref[n * n_kv:(n + 1) * n_kv, :] = per_head / MOBA_BLOCK
        gate = lax.dot_general(q, mean_ref[...], NT_DIMS, precision=lax.Precision.HIGHEST, preferred_element_type=F32)
        row_kv = lax.broadcasted_iota(jnp.int32, gate.shape, 0) % n_kv
        valid = jnp.logical_and(lane_kv_ref[...] == row_kv, lane_blk < nb)
        gm = jnp.where(valid, gate, -jnp.inf)
        rank = jnp.zeros(gm.shape, F32)
        for r in range(1, nb):
            same_side = lane_blk >= r
            other = jnp.where(same_side, pltpu.roll(gm, r * n_kv, 1), pltpu.roll(gm, (r * n_kv - nb * n_kv) % LANES, 1))
            beats = jnp.logical_or(other > gm, jnp.logical_and(other == gm, same_side))
            rank = rank + jnp.where(beats, 1.0, 0.0)
        sel = jnp.logical_and(valid, rank < topk)

        k_new = jnp.concatenate([knew_ref[0]] * group, axis=0)
        v_new = jnp.concatenate([vnew_ref[0]] * group, axis=0)
        s_new = jnp.sum(q * k_new, axis=-1, keepdims=True) * scale
        m_tot = jnp.maximum(s_new, jnp.max(jnp.where(sel, m_all, MASKED), axis=-1, keepdims=True))
        w_all = jnp.where(sel, jnp.exp(m_all - m_tot), 0.0)
        w_new = jnp.exp(s_new - m_tot)
        l_tot = w_new + jnp.sum(w_all * l_all, axis=-1, keepdims=True)
        o_tot = w_new * v_new
        for n in range(nb):
            w_n = jnp.sum(jnp.where(lane_blk == n, w_all, 0.0), axis=-1, keepdims=True)
            o_tot = o_tot + w_n * acc_ref[n]
        o_ref[0] = o_tot / l_tot


def _moba_sample(q, k_new, v_new, cache_k, cache_v, page_table_flat, layer, n_pool, n_pages, page, n_kv):
    nseq, n_q, hd = q.shape
    group = n_q // n_kv
    ppb = MOBA_BLOCK // page
    assert MOBA_BLOCK % page == 0 and n_pages % ppb == 0 and SUBLANES % n_kv == 0 and hd == LANES
    nb = n_pages // ppb
    assert nb * n_kv <= LANES
    pps = _divisor_tile(n_pages, 16, ppb)
    topk = min(MOBA_TOPK, nb + 1)
    rows = np.arange(n_q)[:, None] % n_kv
    cols = np.arange(ppb * page * n_kv)[None, :] % n_kv
    mask = jnp.asarray(np.where(rows == cols, 0.0, MASKED).astype(np.float32))
    lane = np.arange(LANES)
    lane_blk = jnp.asarray((lane // n_kv).astype(np.int32)[None, :])
    lane_kv = jnp.asarray((lane % n_kv).astype(np.int32)[None, :])
    base = layer * n_pool

    def page_map(p):
        return lambda b, j, pt: (base + pt[b * n_pages + j * pps + p], 0, 0)

    per_seq = lambda b, j, pt: (b, 0, 0)
    fixed = lambda b, j, pt: (0, 0)
    cache_spec = [pl.BlockSpec((1, page * n_kv, hd), page_map(p)) for p in range(pps)]
    blocks = (4 * pps * _nbytes((page * n_kv, hd), F32) + pps * _nbytes((page * n_kv, hd), F32)
              + nb * (_nbytes((n_q, hd), F32) + _nbytes((SUBLANES, hd), F32)) + 8 * pps * _nbytes((n_q, page * n_kv), F32))
    kern = functools.partial(_moba_sample_kernel, pages_per_step=pps, ppb=ppb, nb=nb, group=group, n_kv=n_kv, hd=hd,
                             topk=topk, scale=hd ** -0.5)
    return pl.pallas_call(
        kern,
        grid_spec=pltpu.PrefetchScalarGridSpec(
            num_scalar_prefetch=1,
            grid=(nseq, n_pages // pps),
            in_specs=[
                pl.BlockSpec((1, n_q, hd), per_seq), pl.BlockSpec((1, n_kv, hd), per_seq), pl.BlockSpec((1, n_kv, hd), per_seq),
                pl.BlockSpec((n_q, ppb * page * n_kv), fixed), pl.BlockSpec((1, LANES), fixed), pl.BlockSpec((1, LANES), fixed),
            ] + cache_spec + cache_spec,
            out_specs=pl.BlockSpec((1, n_q, hd), per_seq),
            scratch_shapes=[pltpu.VMEM((n_q, LANES), F32), pltpu.VMEM((n_q, LANES), F32), pltpu.VMEM((nb, n_q, hd), F32),
                            pltpu.VMEM((nb, SUBLANES, hd), F32), pltpu.VMEM((LANES, hd), F32)],
        ),
        out_shape=jax.ShapeDtypeStruct((nseq, n_q, hd), F32),
        compiler_params=_params(("parallel", "arbitrary"), blocks),
        name="moba_sample",
    )(page_table_flat, q, k_new, v_new, mask, lane_blk, lane_kv, *([cache_k] * pps), *([cache_v] * pps))


def _mla_in_kernel(x_ref, w_ref, qg_ref, kvg_ref, cos_ref, sin_ref, cq_ref, ckv_ref, ckvb_ref, kr_ref, krb_ref, *, q_lora, kv_lora, half):
    h = jnp.dot(x_ref[...], w_ref[...], preferred_element_type=F32)

    def rms(x, g):
        return x * lax.rsqrt(jnp.mean(x * x, axis=-1, keepdims=True) + RMS_EPS) * g

    cq_ref[...] = rms(h[:, :q_lora], qg_ref[...]).astype(BF16)
    ckv = rms(h[:, q_lora:q_lora + kv_lora], kvg_ref[...])
    ckv_ref[...] = ckv
    ckvb_ref[...] = ckv.astype(BF16)
    kr = h[:, q_lora + kv_lora:]
    kr = kr * cos_ref[...] + _rotate_half(kr, half) * sin_ref[...]
    kr_ref[...] = kr
    krb_ref[...] = kr.astype(BF16)


def _mla_in(x, w, q_norm, kv_norm, cos, sin_signed, q_lora, kv_lora, half):
    n, d = x.shape
    wid = w.shape[1]
    tm = _divisor_tile(n, 640, 16)
    blocks = (2 * (_nbytes((tm, d), BF16) + _nbytes((d, wid), BF16) + 2 * _nbytes((tm, LANES), F32)
                   + _nbytes((tm, q_lora), BF16) + _nbytes((tm, kv_lora), F32) + _nbytes((tm, kv_lora), BF16)
                   + _nbytes((tm, LANES), F32) + _nbytes((tm, LANES), BF16)) + 3 * _nbytes((tm, wid), F32))
    row = lambda i: (i, 0)
    fixed = lambda i: (0, 0)
    return pl.pallas_call(
        functools.partial(_mla_in_kernel, q_lora=q_lora, kv_lora=kv_lora, half=half),
        grid=(n // tm,),
        in_specs=[
            pl.BlockSpec((tm, d), row), pl.BlockSpec((d, wid), fixed), pl.BlockSpec((1, q_lora), fixed),
            pl.BlockSpec((1, kv_lora), fixed), pl.BlockSpec((tm, LANES), row), pl.BlockSpec((tm, LANES), row),
        ],
        out_specs=[pl.BlockSpec((tm, q_lora), row), pl.BlockSpec((tm, kv_lora), row), pl.BlockSpec((tm, kv_lora), row),
                   pl.BlockSpec((tm, LANES), row), pl.BlockSpec((tm, LANES), row)],
        out_shape=[jax.ShapeDtypeStruct((n, q_lora), BF16), jax.ShapeDtypeStruct((n, kv_lora), F32),
                   jax.ShapeDtypeStruct((n, kv_lora), BF16), jax.ShapeDtypeStruct((n, LANES), F32),
                   jax.ShapeDtypeStruct((n, LANES), BF16)],
        compiler_params=_params(("parallel",), blocks),
        name="mla_in",
    )(x, w, q_norm, kv_norm, cos, sin_signed)


def _mla_prompt_kernel(qn_ref, qr_ref, kn_ref, kr_ref, v_ref, o_ref, *, scale):
    i = pl.program_id(2)
    tq = qn_ref.shape[0]
    seq = kn_ref.shape[0]
    q = jnp.concatenate([qn_ref[...], qr_ref[...]], axis=1)
    lower = lax.broadcasted_iota(jnp.int32, (tq, tq), 1) <= lax.broadcasted_iota(jnp.int32, (tq, tq), 0)

    def scores(lo, hi):
        k = jnp.concatenate([kn_ref[lo:hi, :], kr_ref[lo:hi, :]], axis=1)
        return lax.dot_general(q, k, NT_DIMS, preferred_element_type=F32) * scale

    def attend(tile):
        past = tile * tq
        s_own = jnp.where(lower, scores(past, past + tq), MASKED)
        m = jnp.max(s_own, axis=-1, keepdims=True)
        if tile:
            s_past = scores(0, past)
            m = jnp.maximum(m, jnp.max(s_past, axis=-1, keepdims=True))
            p_past = jnp.exp(s_past - m)
        p_own = jnp.exp(s_own - m)
        l = jnp.sum(p_own, axis=-1, keepdims=True)
        o = jnp.dot(p_own.astype(BF16), v_ref[past:past + tq, :], preferred_element_type=F32)
        if tile:
            l = l + jnp.sum(p_past, axis=-1, keepdims=True)
            o = o + jnp.dot(p_past.astype(BF16), v_ref[0:past, :], preferred_element_type=F32)
        o_ref[...] = (o / l).astype(o_ref.dtype)

    for tile in range(seq // tq):
        pl.when(i == tile)(functools.partial(attend, tile))


def _mla_prompt(q2, kv, krb, batch, seq, heads, scale):
    tq = _divisor_tile(seq, 256, 16)
    nqb = seq // tq
    blocks = 2 * (3 * _nbytes((tq, LANES), BF16) + 3 * _nbytes((seq, LANES), BF16)) + 6 * _nbytes((tq, seq), F32)
    return pl.pallas_call(
        functools.partial(_mla_prompt_kernel, scale=scale),
        grid=(batch, heads, nqb),
        in_specs=[
            pl.BlockSpec((tq, LANES), lambda b, h, i: (b * nqb + i, h)),
            pl.BlockSpec((tq, LANES), lambda b, h, i: (b * nqb + i, heads + h)),
            pl.BlockSpec((seq, LANES), lambda b, h, i: (b, h)),
            pl.BlockSpec((seq, LANES), lambda b, h, i: (b, 0)),
            pl.BlockSpec((seq, LANES), lambda b, h, i: (b, heads + h)),
        ],
        out_specs=pl.BlockSpec((tq, LANES), lambda b, h, i: (b * nqb + i, h)),
        out_shape=jax.ShapeDtypeStruct((batch * seq, heads * LANES), BF16),
        compiler_params=_params(("parallel", "parallel", "parallel"), blocks),
        name="mla_prompt",
    )(q2, q2, kv, krb, kv)


def _head_mm_nt_kernel(x_ref, w_ref, o_ref):
    o_ref[...] = lax.dot_general(x_ref[...].astype(BF16), w_ref[...].astype(BF16), NT_DIMS, preferred_element_type=F32)


def _head_mm_kernel(x_ref, w_ref, o_ref):
    o_ref[...] = jnp.dot(x_ref[...].astype(BF16), w_ref[...].astype(BF16), preferred_element_type=F32).astype(o_ref.dtype)


def _mla_q_latent(q2, w_uk2d, row_block, nb, heads, kv_lora):
    blocks = 2 * (_nbytes((nb, LANES), BF16) + _nbytes((kv_lora, LANES), F32) + _nbytes((nb, kv_lora), F32)) * 2
    return pl.pallas_call(
        _head_mm_nt_kernel,
        grid=(heads,),
        in_specs=[pl.BlockSpec((nb, LANES), lambda h: (row_block, h)), pl.BlockSpec((kv_lora, LANES), lambda h: (0, h))],
        out_specs=pl.BlockSpec((nb, kv_lora), lambda h: (0, h)),
        out_shape=jax.ShapeDtypeStruct((nb, heads * kv_lora), F32),
        compiler_params=_params(("parallel",), blocks),
        name="mla_q_latent",
    )(q2, w_uk2d)


def _mla_out_latent(o_lat, w_uv2d, heads, kv_lora):
    nb = o_lat.shape[0]
    blocks = 2 * (_nbytes((nb, kv_lora), F32) + _nbytes((kv_lora, LANES), F32) + _nbytes((nb, LANES), F32)) * 2
    return pl.pallas_call(
        _head_mm_kernel,
        grid=(heads,),
        in_specs=[pl.BlockSpec((nb, kv_lora), lambda h: (0, h)), pl.BlockSpec((kv_lora, LANES), lambda h: (0, h))],
        out_specs=pl.BlockSpec((nb, LANES), lambda h: (0, h)),
        out_shape=jax.ShapeDtypeStruct((nb, heads * LANES), BF16),
        compiler_params=_params(("parallel",), blocks),
        name="mla_out_latent",
    )(o_lat, w_uv2d)


def _mla_sample_kernel(pt_ref, qlat_ref, qr_ref, cnew_ref, rnew_ref, *refs, pages_per_step, rope, scale):
    del pt_ref
    c_refs = refs[:pages_per_step]
    r_refs = refs[pages_per_step:2 * pages_per_step]
    o_ref, m_ref, l_ref, acc_ref = refs[2 * pages_per_step:]
    j = pl.program_id(1)
    qlat = qlat_ref[0]
    qr = qr_ref[0][:, :rope]
    heads = qlat.shape[0]

    @pl.when(j == 0)
    def _():
        c_new = cnew_ref[0]
        s_new = (jnp.sum(qlat * c_new, axis=-1, keepdims=True)
                 + jnp.sum(qr.astype(F32) * rnew_ref[0], axis=-1, keepdims=True)) * scale
        m_ref[...] = jnp.broadcast_to(s_new, m_ref.shape)
        l_ref[...] = jnp.ones_like(l_ref)
        acc_ref[...] = jnp.broadcast_to(c_new, acc_ref.shape)

    qlb = qlat.astype(BF16)
    qrb = qr.astype(BF16)
    page = c_refs[0].shape[1]
    cbs = [c_refs[p][0].astype(BF16) for p in range(pages_per_step)]
    s = jnp.concatenate(
        [lax.dot_general(qlb, cbs[p], NT_DIMS, preferred_element_type=F32)
         + jnp.dot(qrb, r_refs[p][0].astype(BF16), preferred_element_type=F32) for p in range(pages_per_step)],
        axis=-1) * scale
    m_old = m_ref[...][:, 0:1]
    m_new = jnp.maximum(m_old, jnp.max(s, axis=-1, keepdims=True))
    corr = jnp.exp(m_old - m_new)
    e = jnp.exp(s - m_new)
    l_new = l_ref[...][:, 0:1] * corr + jnp.sum(e, axis=-1, keepdims=True)
    eb = e.astype(BF16)
    acc = acc_ref[...] * corr
    for p in range(pages_per_step):
        acc = acc + jnp.dot(eb[:, p * page:(p + 1) * page], cbs[p], preferred_element_type=F32)
    m_ref[...] = jnp.broadcast_to(m_new, m_ref.shape)
    l_ref[...] = jnp.broadcast_to(l_new, l_ref.shape)
    acc_ref[...] = acc

    @pl.when(j == pl.num_programs(1) - 1)
    def _():
        o_ref[0] = acc / l_new


def _mla_sample(qlat, qr, c_new, r_new, lat_cache, rope_cache_t, page_table_flat, layer, n_pool, n_pages, scale):
    nb, heads, kv_lora = qlat.shape
    page = lat_cache.shape[1]
    rope = rope_cache_t.shape[1]
    pps = _divisor_tile(n_pages, 16, 1)
    base = layer * n_pool

    def page_map(p):
        return lambda b, j, pt: (base + pt[b * n_pages + j * pps + p], 0, 0)

    per_seq = lambda b, j, pt: (b, 0, 0)
    blocks = (2 * pps * (_nbytes((page, kv_lora), F32) + _nbytes((rope, page), F32)) + 4 * _nbytes((heads, kv_lora), F32)
              + pps * _nbytes((page, kv_lora), F32))
    kern = functools.partial(_mla_sample_kernel, pages_per_step=pps, rope=rope, scale=scale)
    return pl.pallas_call(
        kern,
        grid_spec=pltpu.PrefetchScalarGridSpec(
            num_scalar_prefetch=1,
            grid=(nb, n_pages // pps),
            in_specs=[
                pl.BlockSpec((1, heads, kv_lora), per_seq), pl.BlockSpec((1, heads, LANES), per_seq),
                pl.BlockSpec((1, 1, kv_lora), per_seq), pl.BlockSpec((1, 1, rope), per_seq),
            ] + [pl.BlockSpec((1, page, kv_lora), page_map(p)) for p in range(pps)]
              + [pl.BlockSpec((1, rope, page), page_map(p)) for p in range(pps)],
            out_specs=pl.BlockSpec((1, heads, kv_lora), per_seq),
            scratch_shapes=[pltpu.VMEM((heads, LANES), F32), pltpu.VMEM((heads, LANES), F32), pltpu.VMEM((heads, kv_lora), F32)],
        ),
        out_shape=jax.ShapeDtypeStruct((nb, heads, kv_lora), F32),
        compiler_params=_params(("parallel", "arbitrary"), blocks),
        name="mla_sample",
    )(page_table_flat, qlat, qr, c_new, r_new, *([lat_cache] * pps), *([rope_cache_t] * pps))


def _rope_tables(pos, dim):
    inv = ROPE_THETA ** (-jnp.arange(0, dim, 2, dtype=F32) / dim)
    ang = pos.astype(F32)[:, None] * inv[None, :]
    ang = jnp.concatenate([ang, ang], axis=-1)
    sign = jnp.where(jnp.arange(dim) < dim // 2, -1.0, 1.0).astype(F32)
    cos, sin = jnp.cos(ang), jnp.sin(ang) * sign[None, :]
    if dim < LANES:
        pad = ((0, 0), (0, LANES - dim))
        cos, sin = jnp.pad(cos, pad), jnp.pad(sin, pad)
    return cos, sin


def kernel(x_prompt, x_sample, state_conv, cache_moba_k, cache_moba_v, cache_mla_latent, cache_mla_krope, page_table, conv_w_in, conv_b_in, conv_w_dw, conv_b_dw, conv_ln_g, conv_ln_b, conv_w_out, conv_b_out, moba_w_qkv, moba_w_o, mla_w_in, mla_q_norm, mla_w_uq, mla_kv_norm, mla_w_uk, mla_w_uv, mla_w_o, mlp_w1, mlp_w2, ln_g, ln_b):
    bp, sp, d = x_prompt.shape
    bs, ts, _ = x_sample.shape
    assert ts == 1 and (bp * sp) % bs == 0
    n_prompt = bp * sp
    depth = mlp_w1.shape[0]
    alpha = (2 * depth) ** 0.25
    n_pages = page_table.shape[1]
    n_pool, page, n_kv, hd = cache_moba_k.shape[1:]
    past_len = n_pages * page
    assert past_len % MOBA_BLOCK == 0
    n_q = moba_w_qkv.shape[2] // hd - 2 * n_kv
    q_lora = mla_q_norm.shape[1]
    kv_lora, heads, nope = mla_w_uk.shape[1:]
    v_dim = mla_w_uv.shape[3]
    rope = cache_mla_krope.shape[3]
    conv_width = conv_w_dw.shape[1]
    assert nope == LANES and v_dim == LANES and rope <= LANES and sp >= conv_width - 1
    mla_scale = (nope + rope) ** -0.5

    pos = jnp.concatenate([jnp.tile(jnp.arange(sp, dtype=jnp.int32), bp), jnp.full((bs,), past_len, jnp.int32)])
    cos_moba, sin_moba = _rope_tables(pos, hd)
    cos_mla, sin_mla = _rope_tables(pos, rope)
    page_table_flat = page_table.reshape(-1).astype(jnp.int32)
    row1 = lambda v: v.reshape(1, -1)

    h = jnp.concatenate([x_prompt.reshape(n_prompt, d), x_sample.reshape(bs, d)], axis=0)
    hb = h.astype(BF16)
    mlp_w1_b, mlp_w2_b = mlp_w1.astype(BF16), mlp_w2.astype(BF16)
    conv_w_in_b, conv_w_out_b = conv_w_in.astype(BF16), conv_w_out.astype(BF16)
    moba_w_o_b, mla_w_o_b = moba_w_o.astype(BF16), mla_w_o.astype(BF16)
    no_bias = jnp.zeros((1, d), F32)

    conv_p, conv_s = [], []
    mk_p, mv_p, mk_s, mv_s = [], [], [], []
    ml_p, mr_p, ml_s, mr_s = [], [], [], []
    for i in range(depth):
        kind, j = i % 3, i // 3
        if kind == 0:
            u = _glu_matmul(hb, conv_w_in_b, j, row1(conv_b_in[j]))
            mixed_p = _conv_prompt(u, conv_w_dw[j], row1(conv_b_dw[j]), row1(conv_ln_g[j]), row1(conv_ln_b[j]), bp, sp)
            prev = state_conv[j].astype(F32)
            mixed_s = _conv_sample(prev.transpose(1, 0, 2), u, n_prompt // bs, conv_w_dw[j], row1(conv_b_dw[j]),
                                   row1(conv_ln_g[j]), row1(conv_ln_b[j]))
            conv_p.append(jnp.stack([u[(b + 1) * sp - (conv_width - 1):(b + 1) * sp] for b in range(bp)]))
            conv_s.append(jnp.concatenate([prev[:, 1:], u[n_prompt:].reshape(bs, 1, -1)], axis=1))
            mixed = jnp.concatenate([mixed_p, mixed_s], axis=0)
            w_out, b_out = conv_w_out_b, row1(conv_b_out[j])
        elif kind == 1:
            nq_cols, nk_cols = n_q * hd, n_kv * hd
            qkv = _matmul_rope(hb, moba_w_qkv[j].astype(BF16), cos_moba, sin_moba, F32, (0, nq_cols + nk_cols), hd // 2,
                               tn=math.gcd(nq_cols, nk_cols))
            mixed_p = _moba_prompt(qkv, bp, sp, n_q, n_kv, hd)
            group = n_q // n_kv
            q_s = qkv[n_prompt:, :nq_cols].reshape(bs, n_kv, group, hd).transpose(0, 2, 1, 3).reshape(bs, n_q, hd)
            k_s = qkv[n_prompt:, nq_cols:nq_cols + nk_cols].reshape(bs, n_kv, hd)
            v_s = qkv[n_prompt:, nq_cols + nk_cols:].reshape(bs, n_kv, hd)
            o_s = _moba_sample(q_s, k_s, v_s,
                               cache_moba_k.reshape(-1, page * n_kv, hd), cache_moba_v.reshape(-1, page * n_kv, hd),
                               page_table_flat, j, n_pool, n_pages, page, n_kv)
            mixed_s = o_s.reshape(bs, group, n_kv, hd).transpose(0, 2, 1, 3).reshape(bs, nq_cols).astype(BF16)
            mk_p.append(qkv[:n_prompt, nq_cols:nq_cols + nk_cols].reshape(bp, sp, n_kv, hd))
            mv_p.append(qkv[:n_prompt, nq_cols + nk_cols:].reshape(bp, sp, n_kv, hd))
            mk_s.append(k_s.reshape(bs, 1, n_kv, hd))
            mv_s.append(v_s.reshape(bs, 1, n_kv, hd))
            mixed = jnp.concatenate([mixed_p, mixed_s], axis=0)
            w_out, b_out = moba_w_o_b, no_bias
        else:
            w_in = jnp.pad(mla_w_in[j], ((0, 0), (0, LANES - rope))).astype(BF16)
            cq, ckv, ckv_b, kr, kr_b = _mla_in(hb, w_in, row1(mla_q_norm[j]), row1(mla_kv_norm[j]), cos_mla, sin_mla,
                                               q_lora, kv_lora, rope // 2)
            w_uq = mla_w_uq[j].reshape(q_lora, heads, nope + rope)
            w_uq_rope = jnp.pad(w_uq[:, :, nope:], ((0, 0), (0, 0), (0, LANES - rope)))
            w_uq2 = jnp.concatenate([w_uq[:, :, :nope].reshape(q_lora, heads * nope),
                                     w_uq_rope.reshape(q_lora, heads * LANES)], axis=1).astype(BF16)
            q2 = _matmul_rope(cq, w_uq2, cos_mla, sin_mla, BF16, (heads * nope, heads * (nope + LANES)), rope // 2,
                              tn=_divisor_tile(heads * LANES, 512, LANES))
            w_uk2d = mla_w_uk[j].reshape(kv_lora, heads * nope)
            w_uv2d = mla_w_uv[j].reshape(kv_lora, heads * v_dim)
            kv = _matmul(ckv_b, jnp.concatenate([w_uk2d, w_uv2d], axis=1).astype(BF16), BF16, rows=n_prompt)
            mixed_p = _mla_prompt(q2, kv, kr_b, bp, sp, heads, mla_scale)
            qlat = _mla_q_latent(q2, w_uk2d, n_prompt // bs, bs, heads, kv_lora).reshape(bs, heads, kv_lora)
            qr_s = q2[n_prompt:, heads * nope:].reshape(bs, heads, LANES)
            c_new = ckv[n_prompt:].reshape(bs, 1, kv_lora)
            r_new = kr[n_prompt:, :rope].reshape(bs, 1, rope)
            o_lat = _mla_sample(qlat, qr_s, c_new, r_new,
                                cache_mla_latent.reshape(-1, page, kv_lora),
                                cache_mla_krope.transpose(0, 1, 3, 2).reshape(-1, rope, page),
                                page_table_flat, j, cache_mla_latent.shape[1], n_pages, mla_scale)
            mixed_s = _mla_out_latent(o_lat.reshape(bs, heads * kv_lora), w_uv2d, heads, kv_lora)
            ml_p.append(ckv[:n_prompt].reshape(bp, sp, kv_lora))
            mr_p.append(kr[:n_prompt, :rope].reshape(bp, sp, rope))
            ml_s.append(c_new)
            mr_s.append(r_new)
            mixed = jnp.concatenate([mixed_p, mixed_s], axis=0)
            w_out, b_out = mla_w_o_b, no_bias
        h, hb = _matmul_res_ln(mixed, w_out, j, b_out, h, row1(ln_g[i, 0]), row1(ln_b[i, 0]), alpha)
        h, hb = _mlp_res_ln(hb, h, mlp_w1_b, mlp_w2_b, i, row1(ln_g[i, 1]), row1(ln_b[i, 1]), alpha)

    return (h[:n_prompt].reshape(bp, sp, d), h[n_prompt:].reshape(bs, 1, d),
            jnp.stack(conv_p), jnp.stack(conv_s),
            jnp.stack(mk_p), jnp.stack(mv_p), jnp.stack(mk_s), jnp.stack(mv_s),
            jnp.stack(ml_p), jnp.stack(mr_p), jnp.stack(ml_s), jnp.stack(mr_s))
```

```python
import functools
import math

import numpy as np
import jax
import jax.numpy as jnp
from jax import lax
from jax.experimental import pallas as pl
from jax.experimental.pallas import tpu as pltpu

F32 = jnp.float32
BF16 = jnp.bfloat16

MOBA_BLOCK = 256
MOBA_TOPK = 3
ROPE_THETA = 10000.0
LN_EPS = 1e-5
RMS_EPS = 1e-6
MASKED = -1e30

LANES = 128
SUBLANES = 8
VMEM_CAP_BYTES = 60 * 1024 * 1024
VMEM_FLOOR_BYTES = 32 * 1024 * 1024

NT_DIMS = (((1,), (1,)), ((), ()))


def _divisor_tile(n, target, mult):
    best = None
    for t in range(mult, min(n, target) + 1, mult):
        if n % t == 0:
            best = t
    return n if best is None else best


def _params(semantics, block_bytes):
    limit = int(min(max(block_bytes * 1.3 + (6 << 20), VMEM_FLOOR_BYTES), VMEM_CAP_BYTES))
    return pltpu.CompilerParams(dimension_semantics=semantics, vmem_limit_bytes=limit)


def _nbytes(shape, dtype):
    return int(np.prod(shape)) * jnp.dtype(dtype).itemsize


def _layer_norm(z, g, b):
    mu = jnp.mean(z, axis=-1, keepdims=True)
    zc = z - mu
    var = jnp.mean(zc * zc, axis=-1, keepdims=True)
    return zc * lax.rsqrt(var + LN_EPS) * g + b


def _sigmoid(x):
    return 1.0 / (1.0 + jnp.exp(-x))


class _PageStream:
    def __init__(self, pt_ref, pages_per_step, base, streams):
        steps = pl.num_programs(1)
        self.pt_ref, self.pages, self.base, self.streams = pt_ref, pages_per_step, base, streams
        self.step = pl.program_id(0) * steps + pl.program_id(1)
        self.last = pl.num_programs(0) * steps - 1
        self.slot = lax.rem(self.step, 2)
        self.next_step = jnp.minimum(self.step + 1, self.last)

    def _copies(self, step, slot, p):
        page = self.base + self.pt_ref[step * self.pages + p]
        return [pltpu.make_async_copy(hbm.at[page], buf.at[slot, p], sem.at[slot, p]) for hbm, buf, sem in self.streams]

    def begin_step(self):
        @pl.when(self.step == 0)
        def _():
            for p in range(self.pages):
                for copy in self._copies(self.step, self.slot, p):
                    copy.start()

        for p in range(self.pages):
            for copy in self._copies(self.step, self.slot, p):
                copy.wait()
        return self.slot

    def prefetch_next(self, p):
        for copy in self._copies(self.next_step, 1 - self.slot, p):
            copy.start()

    def end_step(self):
        @pl.when(self.step == self.last)
        def _():
            for p in range(self.pages):
                for copy in self._copies(self.next_step, 1 - self.slot, p):
                    copy.wait()


def _mm_kernel(x_ref, w_ref, o_ref):
    o_ref[...] = jnp.dot(x_ref[...], w_ref[...], preferred_element_type=F32).astype(o_ref.dtype)


def _matmul(x, w, out_dtype, rows=None, tm_target=1024, tn_target=512):
    rows = x.shape[0] if rows is None else rows
    k, nout = w.shape
    tm = _divisor_tile(rows, tm_target, 16)
    tn = _divisor_tile(nout, tn_target, LANES)
    blocks = 2 * (_nbytes((tm, k), BF16) + _nbytes((k, tn), BF16) + _nbytes((tm, tn), out_dtype)) + _nbytes((tm, tn), F32)
    return pl.pallas_call(
        _mm_kernel,
        grid=(rows // tm, nout // tn),
        in_specs=[pl.BlockSpec((tm, k), lambda i, j: (i, 0)), pl.BlockSpec((k, tn), lambda i, j: (0, j))],
        out_specs=pl.BlockSpec((tm, tn), lambda i, j: (i, j)),
        out_shape=jax.ShapeDtypeStruct((rows, nout), out_dtype),
        compiler_params=_params(("parallel", "parallel"), blocks),
        name="matmul",
    )(x, w)


def _glu_kernel(x_ref, wa_ref, wg_ref, ba_ref, bg_ref, o_ref):
    x = x_ref[...]
    a = jnp.dot(x, wa_ref[...], preferred_element_type=F32) + ba_ref[...]
    g = jnp.dot(x, wg_ref[...], preferred_element_type=F32) + bg_ref[...]
    o_ref[...] = a * _sigmoid(g)


def _glu_matmul(x, w, layer, b):
    n, k = x.shape
    c = w.shape[2] // 2
    tm = _divisor_tile(n, 1024, 16)
    tn = _divisor_tile(c, 512, LANES)
    nj = c // tn
    blocks = 2 * (_nbytes((tm, k), BF16) + 2 * _nbytes((k, tn), BF16) + _nbytes((tm, tn), F32)) + 3 * _nbytes((tm, tn), F32)
    return pl.pallas_call(
        _glu_kernel,
        grid=(n // tm, nj),
        in_specs=[
            pl.BlockSpec((tm, k), lambda i, j: (i, 0)),
            pl.BlockSpec((None, k, tn), lambda i, j: (layer, 0, j)),
            pl.BlockSpec((None, k, tn), lambda i, j: (layer, 0, nj + j)),
            pl.BlockSpec((1, tn), lambda i, j: (0, j)),
            pl.BlockSpec((1, tn), lambda i, j: (0, nj + j)),
        ],
        out_specs=pl.BlockSpec((tm, tn), lambda i, j: (i, j)),
        out_shape=jax.ShapeDtypeStruct((n, c), F32),
        compiler_params=_params(("parallel", "parallel"), blocks),
        name="glu_matmul",
    )(x, w, w, b, b)


def _rotate_half(a, half):
    if 2 * half == LANES:
        return pltpu.roll(a, half, 1)
    lane = lax.broadcasted_iota(jnp.int32, a.shape, 1)
    return jnp.where(lane % (2 * half) < half, pltpu.roll(a, LANES - half, 1), pltpu.roll(a, half, 1))


def _mm_rope_kernel(x_ref, w_ref, cos_ref, sin_ref, o_ref, *, rope_lo, rope_hi, half):
    acc = jnp.dot(x_ref[...], w_ref[...], preferred_element_type=F32)
    j = pl.program_id(1)
    in_rope = jnp.logical_and(j >= rope_lo, j < rope_hi)

    @pl.when(in_rope)
    def _():
        cos = cos_ref[...]
        sin = sin_ref[...]
        for c0 in range(0, acc.shape[1], LANES):
            a = acc[:, c0:c0 + LANES]
            o_ref[:, c0:c0 + LANES] = (a * cos + _rotate_half(a, half) * sin).astype(o_ref.dtype)

    @pl.when(jnp.logical_not(in_rope))
    def _():
        o_ref[...] = acc.astype(o_ref.dtype)


def _matmul_rope(x, w, cos, sin_signed, out_dtype, rope_cols, half, tn):
    n, k = x.shape
    nout = w.shape[1]
    tm = _divisor_tile(n, 1024, 16)
    assert nout % tn == 0 and rope_cols[0] % tn == 0 and rope_cols[1] % tn == 0
    blocks = 2 * (_nbytes((tm, k), BF16) + _nbytes((k, tn), BF16) + 2 * _nbytes((tm, LANES), F32)
                  + _nbytes((tm, tn), out_dtype)) + 2 * _nbytes((tm, tn), F32)
    kern = functools.partial(_mm_rope_kernel, rope_lo=rope_cols[0] // tn, rope_hi=rope_cols[1] // tn, half=half)
    return pl.pallas_call(
        kern,
        grid=(n // tm, nout // tn),
        in_specs=[
            pl.BlockSpec((tm, k), lambda i, j: (i, 0)),
            pl.BlockSpec((k, tn), lambda i, j: (0, j)),
            pl.BlockSpec((tm, LANES), lambda i, j: (i, 0)),
            pl.BlockSpec((tm, LANES), lambda i, j: (i, 0)),
        ],
        out_specs=pl.BlockSpec((tm, tn), lambda i, j: (i, j)),
        out_shape=jax.ShapeDtypeStruct((n, nout), out_dtype),
        compiler_params=_params(("parallel", "parallel"), blocks),
        name="matmul_rope",
    )(x, w, cos, sin_signed)


def _mm_res_ln_kernel(x_ref, w_ref, bias_ref, res_ref, g_ref, b_ref, of_ref, ob_ref, *, alpha):
    y = jnp.dot(x_ref[...], w_ref[...], preferred_element_type=F32) + bias_ref[...]
    out = _layer_norm(alpha * res_ref[...] + y, g_ref[...], b_ref[...])
    of_ref[...] = out
    ob_ref[...] = out.astype(BF16)


def _matmul_res_ln(x, w, layer, bias, res, g, b, alpha):
    n, k = x.shape
    d = w.shape[2]
    tm = _divisor_tile(n, 640, 16)
    blocks = (2 * (_nbytes((tm, k), BF16) + _nbytes((k, d), BF16) + 2 * _nbytes((tm, d), F32) + _nbytes((tm, d), BF16))
              + 3 * _nbytes((tm, d), F32))
    row = lambda i: (i, 0)
    fixed = lambda i: (0, 0)
    return pl.pallas_call(
        functools.partial(_mm_res_ln_kernel, alpha=alpha),
        grid=(n // tm,),
        in_specs=[
            pl.BlockSpec((tm, k), row), pl.BlockSpec((None, k, d), lambda i: (layer, 0, 0)), pl.BlockSpec((1, d), fixed),
            pl.BlockSpec((tm, d), row), pl.BlockSpec((1, d), fixed), pl.BlockSpec((1, d), fixed),
        ],
        out_specs=[pl.BlockSpec((tm, d), row), pl.BlockSpec((tm, d), row)],
        out_shape=[jax.ShapeDtypeStruct((n, d), F32), jax.ShapeDtypeStruct((n, d), BF16)],
        compiler_params=_params(("parallel",), blocks),
        name="matmul_res_ln",
    )(x, w, bias, res, g, b)


def _mlp_kernel(xb_ref, xf_ref, w1_ref, w2_ref, g_ref, b_ref, of_ref, ob_ref, acc_ref, *, alpha):
    f = pl.program_id(1)

    @pl.when(f == 0)
    def _():
        acc_ref[...] = jnp.zeros_like(acc_ref)

    h1 = jnp.maximum(jnp.dot(xb_ref[...], w1_ref[...], preferred_element_type=F32), 0.0)
    acc_ref[...] += jnp.dot((h1 * h1).astype(BF16), w2_ref[...], preferred_element_type=F32)

    @pl.when(f == pl.num_programs(1) - 1)
    def _():
        out = _layer_norm(alpha * xf_ref[...] + acc_ref[...], g_ref[...], b_ref[...])
        of_ref[...] = out
        ob_ref[...] = out.astype(BF16)


def _mlp_res_ln(xb, xf, w1, w2, layer, g, b, alpha):
    n, d = xb.shape
    ff = w1.shape[2]
    tm = _divisor_tile(n, 640, 16)
    tf = _divisor_tile(ff, 512, LANES)
    blocks = (2 * (_nbytes((tm, d), BF16) * 2 + 2 * _nbytes((tm, d), F32) + _nbytes((d, tf), BF16) + _nbytes((tf, d), BF16))
              + _nbytes((tm, d), F32) + 2 * _nbytes((tm, tf), F32))
    row = lambda i, f: (i, 0)
    fixed = lambda i, f: (0, 0)
    return pl.pallas_call(
        functools.partial(_mlp_kernel, alpha=alpha),
        grid=(n // tm, ff // tf),
        in_specs=[
            pl.BlockSpec((tm, d), row), pl.BlockSpec((tm, d), row),
            pl.BlockSpec((None, d, tf), lambda i, f: (layer, 0, f)), pl.BlockSpec((None, tf, d), lambda i, f: (layer, f, 0)),
            pl.BlockSpec((1, d), fixed), pl.BlockSpec((1, d), fixed),
        ],
        out_specs=[pl.BlockSpec((tm, d), row), pl.BlockSpec((tm, d), row)],
        out_shape=[jax.ShapeDtypeStruct((n, d), F32), jax.ShapeDtypeStruct((n, d), BF16)],
        scratch_shapes=[pltpu.VMEM((tm, d), F32)],
        compiler_params=_params(("parallel", "arbitrary"), blocks),
        name="mlp_res_ln",
    )(xb, xf, w1, w2, g, b)


CONV_HALO = 32
CONV_ROWS = 128
CONV_COLS = 128


def _conv_prompt_kernel(halo_ref, u_ref, w_ref, bdw_ref, g_ref, b_ref, o_ref, win_ref, sh_ref, c_ref, *, width, tt):
    first = pl.program_id(1) == 0
    win_ref[0:CONV_HALO, :] = jnp.where(first, 0.0, halo_ref[...])
    win_ref[CONV_HALO:CONV_HALO + tt, :] = u_ref[...]
    chans = u_ref.shape[1]
    off0 = CONV_HALO - (width - 1)
    nrows = sh_ref.shape[1]
    for s in range(1, SUBLANES):
        sh_ref[s - 1] = win_ref[s:s + nrows, :]
    def chunk(ci, carry):
        cs = pl.ds(pl.multiple_of(ci * CONV_COLS, CONV_COLS), CONV_COLS)
        for r0 in range(0, tt, CONV_ROWS):
            acc = jnp.zeros((CONV_ROWS, CONV_COLS), F32)
            for k in range(width):
                s = (off0 + k) % SUBLANES
                row = r0 + off0 + k - s
                if s:
                    x = sh_ref[s - 1, row:row + CONV_ROWS, cs]
                else:
                    x = win_ref[row:row + CONV_ROWS, cs]
                acc = acc + w_ref[k:k + 1, cs] * x
            c_ref[r0:r0 + CONV_ROWS, cs] = acc + bdw_ref[:, cs]
        return carry

    lax.fori_loop(0, chans // CONV_COLS, chunk, 0)
    y = _layer_norm(c_ref[...], g_ref[...], b_ref[...])
    o_ref[...] = (y * _sigmoid(y)).astype(o_ref.dtype)


def _conv_prompt(u, w_dw, b_dw, g, b, batch, seq):
    chans = u.shape[1]
    width = w_dw.shape[0]
    tt = _divisor_tile(seq, 128, CONV_ROWS)
    assert width - 1 <= CONV_HALO and tt % CONV_HALO == 0 and tt % CONV_ROWS == 0 and chans % CONV_COLS == 0
    nt = seq // tt
    per = tt // CONV_HALO
    shift_rows = tt + (CONV_HALO - SUBLANES)
    blocks = (2 * (_nbytes((CONV_HALO, chans), F32) + _nbytes((tt, chans), F32) + _nbytes((width, chans), F32)
                   + _nbytes((tt, chans), BF16)) + _nbytes((tt + CONV_HALO, chans), F32) + 3 * _nbytes((tt, chans), F32)
              + (SUBLANES - 1) * _nbytes((shift_rows, chans), F32))
    fixed = lambda bi, i: (0, 0)
    return pl.pallas_call(
        functools.partial(_conv_prompt_kernel, width=width, tt=tt),
        grid=(batch, nt),
        in_specs=[
            pl.BlockSpec((CONV_HALO, chans), lambda bi, i: (jnp.maximum((bi * nt + i) * per - 1, 0), 0)),
            pl.BlockSpec((tt, chans), lambda bi, i: (bi * nt + i, 0)),
            pl.BlockSpec((width, chans), fixed), pl.BlockSpec((1, chans), fixed),
            pl.BlockSpec((1, chans), fixed), pl.BlockSpec((1, chans), fixed),
        ],
        out_specs=pl.BlockSpec((tt, chans), lambda bi, i: (bi * nt + i, 0)),
        out_shape=jax.ShapeDtypeStruct((batch * seq, chans), BF16),
        scratch_shapes=[pltpu.VMEM((tt + CONV_HALO, chans), F32), pltpu.VMEM((SUBLANES - 1, shift_rows, chans), F32),
                        pltpu.VMEM((tt, chans), F32)],
        compiler_params=_params(("parallel", "arbitrary"), blocks),
        name="conv_prompt",
    )(u, u, w_dw, b_dw, g, b)


def _conv_sample_kernel(prev_ref, u_ref, w_ref, bdw_ref, g_ref, b_ref, o_ref):
    taps = prev_ref.shape[0]
    c = u_ref[...] * w_ref[taps:taps + 1, :] + bdw_ref[...]
    for k in range(taps):
        c = c + prev_ref[k] * w_ref[k:k + 1, :]
    y = _layer_norm(c, g_ref[...], b_ref[...])
    o_ref[...] = (y * _sigmoid(y)).astype(o_ref.dtype)


def _conv_sample(prev_t, u, row_block, w_dw, b_dw, g, b):
    taps, nb, chans = prev_t.shape
    tb = _divisor_tile(nb, 16, 16)
    per = nb // tb
    blocks = 2 * (_nbytes((taps, tb, chans), F32) + _nbytes((taps + 1, chans), F32) + 2 * _nbytes((tb, chans), F32)) + 3 * _nbytes((tb, chans), F32)
    fixed = lambda i: (0, 0)
    return pl.pallas_call(
        _conv_sample_kernel,
        grid=(per,),
        in_specs=[
            pl.BlockSpec((taps, tb, chans), lambda i: (0, i, 0)),
            pl.BlockSpec((tb, chans), lambda i: (row_block * per + i, 0)),
            pl.BlockSpec((taps + 1, chans), fixed), pl.BlockSpec((1, chans), fixed),
            pl.BlockSpec((1, chans), fixed), pl.BlockSpec((1, chans), fixed),
        ],
        out_specs=pl.BlockSpec((tb, chans), lambda i: (i, 0)),
        out_shape=jax.ShapeDtypeStruct((nb, chans), BF16),
        compiler_params=_params(("parallel",), blocks),
        name="conv_sample",
    )(prev_t, u, w_dw, b_dw, g, b)


def _moba_prompt_kernel(q_ref, k_ref, v_ref, e_ref, nidx_ref, o_ref, kb_ref, vb_ref, mean_ref, *, group, hd, blk, nbk, topk, scale):
    i = pl.program_id(2)

    @pl.when(i == 0)
    def _():
        k = k_ref[...]
        kb_ref[...] = k.astype(BF16)
        vb_ref[...] = v_ref[...].astype(BF16)
        mean_ref[...] = jnp.zeros_like(mean_ref)
        for n in range(nbk):
            mean_n = jnp.sum(k[n * blk:(n + 1) * blk], axis=0, keepdims=True) / blk
            for g in range(group):
                mean_ref[g * nbk + n:g * nbk + n + 1, g * hd:(g + 1) * hd] = mean_n

    q = q_ref[...]
    gate = lax.dot_general(q, mean_ref[...], NT_DIMS, precision=lax.Precision.HIGHEST, preferred_element_type=F32)
    nidx = nidx_ref[...]
    past = nidx < i
    gm = jnp.where(past, gate, -jnp.inf)
    rank = jnp.zeros(gm.shape, F32)
    for r in range(1, nbk):
        same_side = nidx >= r
        other = jnp.where(same_side, pltpu.roll(gm, r, 1), pltpu.roll(gm, (r - nbk) % LANES, 1))
        beats = jnp.logical_or(other > gm, jnp.logical_and(other == gm, same_side))
        rank = rank + jnp.where(beats, 1.0, 0.0)
    selected = jnp.logical_and(past, rank < topk)
    sel_bias = jnp.where(selected, 0.0, MASKED).astype(BF16)
    qb = q.astype(BF16)
    lower = lax.broadcasted_iota(jnp.int32, (blk, blk), 1) <= lax.broadcasted_iota(jnp.int32, (blk, blk), 0)

    def attend(own):
        past_keys = own * blk
        k_own = kb_ref[past_keys:past_keys + blk, :]
        v_own = vb_ref[past_keys:past_keys + blk, :]
        for g in range(group):
            qg = qb[:, g * hd:(g + 1) * hd]
            s_own = jnp.where(lower, lax.dot_general(qg, k_own, NT_DIMS, preferred_element_type=F32) * scale, MASKED)
            m = jnp.max(s_own, axis=-1, keepdims=True)
            if own:
                bias = jnp.dot(sel_bias, e_ref[g, :, 0:past_keys], preferred_element_type=F32)
                s_past = lax.dot_general(qg, kb_ref[0:past_keys, :], NT_DIMS, preferred_element_type=F32) * scale + bias
                m = jnp.maximum(m, jnp.max(s_past, axis=-1, keepdims=True))
                p_past = jnp.exp(s_past - m)
            p_own = jnp.exp(s_own - m)
            l = jnp.sum(p_own, axis=-1, keepdims=True)
            o = jnp.dot(p_own.astype(BF16), v_own, preferred_element_type=F32)
            if own:
                l = l + jnp.sum(p_past, axis=-1, keepdims=True)
                o = o + jnp.dot(p_past.astype(BF16), vb_ref[0:past_keys, :], preferred_element_type=F32)
            o_ref[:, g * hd:(g + 1) * hd] = (o / l).astype(o_ref.dtype)

    for own in range(nbk):
        pl.when(i == own)(functools.partial(attend, own))


def _moba_prompt(qkv, batch, seq, n_q, n_kv, hd):
    blk = MOBA_BLOCK
    assert seq % blk == 0 and hd == LANES
    nbk = seq // blk
    group = n_q // n_kv
    assert group * nbk <= LANES
    topk = min(MOBA_TOPK, nbk)
    lane = np.arange(LANES)
    nidx = np.where(lane < group * nbk, lane % nbk, 1 << 20).astype(np.int32)[None, :]
    spread = np.zeros((group, LANES, seq), np.float32)
    for g in range(group):
        for n in range(nbk):
            spread[g, g * nbk + n, n * blk:(n + 1) * blk] = 1.0
    nqb = seq // blk
    blocks = (2 * (_nbytes((blk, group * hd), F32) + 2 * _nbytes((seq, hd), F32) + _nbytes((group, LANES, seq), BF16)
                   + _nbytes((blk, group * hd), BF16)) + 2 * _nbytes((seq, hd), BF16) + _nbytes((LANES, group * hd), F32)
              + 6 * _nbytes((blk, seq), F32))
    kern = functools.partial(_moba_prompt_kernel, group=group, hd=hd, blk=blk, nbk=nbk, topk=topk, scale=hd ** -0.5)
    return pl.pallas_call(
        kern,
        grid=(batch, n_kv, nqb),
        in_specs=[
            pl.BlockSpec((blk, group * hd), lambda b, h, i: (b * nqb + i, h)),
            pl.BlockSpec((seq, hd), lambda b, h, i: (b, n_q + h)),
            pl.BlockSpec((seq, hd), lambda b, h, i: (b, n_q + n_kv + h)),
            pl.BlockSpec((group, LANES, seq), lambda b, h, i: (0, 0, 0)),
            pl.BlockSpec((1, LANES), lambda b, h, i: (0, 0)),
        ],
        out_specs=pl.BlockSpec((blk, group * hd), lambda b, h, i: (b * nqb + i, h)),
        out_shape=jax.ShapeDtypeStruct((batch * seq, n_q * hd), BF16),
        scratch_shapes=[pltpu.VMEM((seq, hd), BF16), pltpu.VMEM((seq, hd), BF16), pltpu.VMEM((LANES, group * hd), F32)],
        compiler_params=_params(("parallel", "parallel", "arbitrary"), blocks),
        name="moba_prompt",
    )(qkv, qkv, qkv, jnp.asarray(spread, BF16), jnp.asarray(nidx))


def _rowsum8(x):
    parts = x.reshape(x.shape[0] // SUBLANES, SUBLANES, x.shape[1])
    while parts.shape[0] > 1 and parts.shape[0] % 2 == 0:
        half = parts.shape[0] // 2
        parts = parts[:half] + parts[half:]
    return jnp.sum(parts, axis=0)


def _moba_sample_kernel(pt_ref, q_ref, knew_ref, vnew_ref, mask_ref, lane_blk_ref, lane_kv_ref, k_hbm, v_hbm, o_ref, m_ref, l_ref, acc_ref, ksum_ref, mean_ref, kbuf, vbuf, ksem, vsem, *, pages_per_step, base, ppb, nb, group, n_kv, hd, topk, scale):
    j = pl.program_id(1)
    stream = _PageStream(pt_ref, pages_per_step, base, ((k_hbm, kbuf, ksem), (v_hbm, vbuf, vsem)))
    slot = stream.begin_step()
    q = q_ref[0]
    qb = q.astype(BF16)
    mask = mask_ref[...]
    n_q = q.shape[0]
    lane_blk = lane_blk_ref[...]
    bps = pages_per_step // ppb

    @pl.when(j == 0)
    def _():
        m_ref[...] = jnp.zeros_like(m_ref)
        l_ref[...] = jnp.zeros_like(l_ref)

    kps, scores = [], []
    for p in range(pages_per_step):
        stream.prefetch_next(p)
        kps.append(kbuf[slot, p])
        scores.append(lax.dot_general(qb, kps[p].astype(BF16), NT_DIMS, preferred_element_type=F32))
    m_all = m_ref[...]
    l_all = l_ref[...]
    for bi in range(bps):
        blk = j * bps + bi
        s = jnp.concatenate(scores[bi * ppb:(bi + 1) * ppb], axis=-1) * scale + mask
        m = jnp.max(s, axis=-1, keepdims=True)
        e = jnp.exp(s - m)
        l = jnp.sum(e, axis=-1, keepdims=True)
        eb = e.astype(BF16)
        cols = s.shape[1] // ppb
        acc = jnp.dot(eb[:, 0:cols], vbuf[slot, bi * ppb].astype(BF16), preferred_element_type=F32)
        ksum = _rowsum8(kps[bi * ppb])
        for t in range(1, ppb):
            acc = acc + jnp.dot(eb[:, t * cols:(t + 1) * cols], vbuf[slot, bi * ppb + t].astype(BF16), preferred_element_type=F32)
            ksum = ksum + _rowsum8(kps[bi * ppb + t])
        hit = lane_blk == blk
        m_all = jnp.where(hit, m, m_all)
        l_all = jnp.where(hit, l, l_all)
        acc_ref[blk] = acc
        ksum_ref[blk] = ksum
    m_ref[...] = m_all
    l_ref[...] = l_all

    @pl.when(j == pl.num_programs(1) - 1)
    def _():
        mean_ref[...] = jnp.zeros_like(mean_ref)
        for n in range(nb):
            bs = ksum_ref[n]
            per_head = bs[0:n_kv]
            for t in range(1, SUBLANES // n_kv):
                per_head = per_head + bs[t * n_kv:(t + 1) * n_kv]
            mean_ref[n * n_kv:(n + 1) * n_kv, :] = per_head / MOBA_BLOCK
        gate = lax.dot_general(q, mean_ref[...], NT_DIMS, precision=lax.Precision.HIGHEST, preferred_element_type=F32)
        row_kv = lax.broadcasted_iota(jnp.int32, gate.shape, 0) % n_kv
        valid = jnp.logical_and(lane_kv_ref[...] == row_kv, lane_blk < nb)
        gm = jnp.where(valid, gate, -jnp.inf)
        rank = jnp.zeros(gm.shape, F32)
        for r in range(1, nb):
            same_side = lane_blk >= r
            other = jnp.where(same_side, pltpu.roll(gm, r * n_kv, 1), pltpu.roll(gm, (r * n_kv - nb * n_kv) % LANES, 1))
            beats = jnp.logical_or(other > gm, jnp.logical_and(other == gm, same_side))
            rank = rank + jnp.where(beats, 1.0, 0.0)
        sel = jnp.logical_and(valid, rank < topk)

        k_new = jnp.concatenate([knew_ref[0]] * group, axis=0)
        v_new = jnp.concatenate([vnew_ref[0]] * group, axis=0)
        s_new = jnp.sum(q * k_new, axis=-1, keepdims=True) * scale
        m_tot = jnp.maximum(s_new, jnp.max(jnp.where(sel, m_all, MASKED), axis=-1, keepdims=True))
        w_all = jnp.where(sel, jnp.exp(m_all - m_tot), 0.0)
        w_new = jnp.exp(s_new - m_tot)
        l_tot = w_new + jnp.sum(w_all * l_all, axis=-1, keepdims=True)
        o_tot = w_new * v_new
        for n in range(nb):
            w_n = jnp.sum(jnp.where(lane_blk == n, w_all, 0.0), axis=-1, keepdims=True)
            o_tot = o_tot + w_n * acc_ref[n]
        o_ref[0] = o_tot / l_tot

    stream.end_step()


def _moba_sample(q, k_new, v_new, cache_k, cache_v, page_table_flat, layer, n_pool, n_pages, page, n_kv):
    nseq, n_q, hd = q.shape
    group = n_q // n_kv
    ppb = MOBA_BLOCK // page
    assert MOBA_BLOCK % page == 0 and n_pages % ppb == 0 and SUBLANES % n_kv == 0 and hd == LANES
    nb = n_pages // ppb
    assert nb * n_kv <= LANES
    pps = _divisor_tile(n_pages, 16, ppb)
    topk = min(MOBA_TOPK, nb + 1)
    rows = np.arange(n_q)[:, None] % n_kv
    cols = np.arange(ppb * page * n_kv)[None, :] % n_kv
    mask = jnp.asarray(np.where(rows == cols, 0.0, MASKED).astype(np.float32))
    lane = np.arange(LANES)
    lane_blk = jnp.asarray((lane // n_kv).astype(np.int32)[None, :])
    lane_kv = jnp.asarray((lane % n_kv).astype(np.int32)[None, :])
    per_seq = lambda b, j, pt: (b, 0, 0)
    fixed = lambda b, j, pt: (0, 0)
    blocks = (4 * pps * _nbytes((page * n_kv, hd), F32) + pps * _nbytes((page * n_kv, hd), F32)
              + nb * (_nbytes((n_q, hd), F32) + _nbytes((SUBLANES, hd), F32)) + 8 * pps * _nbytes((n_q, page * n_kv), F32))
    kern = functools.partial(_moba_sample_kernel, pages_per_step=pps, base=layer * n_pool, ppb=ppb, nb=nb, group=group,
                             n_kv=n_kv, hd=hd, topk=topk, scale=hd ** -0.5)
    return pl.pallas_call(
        kern,
        grid_spec=pltpu.PrefetchScalarGridSpec(
            num_scalar_prefetch=1,
            grid=(nseq, n_pages // pps),
            in_specs=[
                pl.BlockSpec((1, n_q, hd), per_seq), pl.BlockSpec((1, n_kv, hd), per_seq), pl.BlockSpec((1, n_kv, hd), per_seq),
                pl.BlockSpec((n_q, ppb * page * n_kv), fixed), pl.BlockSpec((1, LANES), fixed), pl.BlockSpec((1, LANES), fixed),
                pl.BlockSpec(memory_space=pl.ANY), pl.BlockSpec(memory_space=pl.ANY),
            ],
            out_specs=pl.BlockSpec((1, n_q, hd), per_seq),
            scratch_shapes=[pltpu.VMEM((n_q, LANES), F32), pltpu.VMEM((n_q, LANES), F32), pltpu.VMEM((nb, n_q, hd), F32),
                            pltpu.VMEM((nb, SUBLANES, hd), F32), pltpu.VMEM((LANES, hd), F32),
                            pltpu.VMEM((2, pps, page * n_kv, hd), F32), pltpu.VMEM((2, pps, page * n_kv, hd), F32),
                            pltpu.SemaphoreType.DMA((2, pps)), pltpu.SemaphoreType.DMA((2, pps))],
        ),
        out_shape=jax.ShapeDtypeStruct((nseq, n_q, hd), F32),
        compiler_params=_params(("arbitrary", "arbitrary"), blocks),
        name="moba_sample",
    )(page_table_flat, q, k_new, v_new, mask, lane_blk, lane_kv, cache_k, cache_v)


def _mla_in_kernel(x_ref, w_ref, qg_ref, kvg_ref, cos_ref, sin_ref, cq_ref, ckv_ref, ckvb_ref, kr_ref, krb_ref, *, q_lora, kv_lora, half):
    h = jnp.dot(x_ref[...], w_ref[...], preferred_element_type=F32)

    def rms(x, g):
        return x * lax.rsqrt(jnp.mean(x * x, axis=-1, keepdims=True) + RMS_EPS) * g

    cq_ref[...] = rms(h[:, :q_lora], qg_ref[...]).astype(BF16)
    ckv = rms(h[:, q_lora:q_lora + kv_lora], kvg_ref[...])
    ckv_ref[...] = ckv
    ckvb_ref[...] = ckv.astype(BF16)
    kr = h[:, q_lora + kv_lora:]
    kr = kr * cos_ref[...] + _rotate_half(kr, half) * sin_ref[...]
    kr_ref[...] = kr
    krb_ref[...] = kr.astype(BF16)


def _mla_in(x, w, q_norm, kv_norm, cos, sin_signed, q_lora, kv_lora, half):
    n, d = x.shape
    wid = w.shape[1]
    tm = _divisor_tile(n, 640, 16)
    blocks = (2 * (_nbytes((tm, d), BF16) + _nbytes((d, wid), BF16) + 2 * _nbytes((tm, LANES), F32)
                   + _nbytes((tm, q_lora), BF16) + _nbytes((tm, kv_lora), F32) + _nbytes((tm, kv_lora), BF16)
                   + _nbytes((tm, LANES), F32) + _nbytes((tm, LANES), BF16)) + 3 * _nbytes((tm, wid), F32))
    row = lambda i: (i, 0)
    fixed = lambda i: (0, 0)
    return pl.pallas_call(
        functools.partial(_mla_in_kernel, q_lora=q_lora, kv_lora=kv_lora, half=half),
        grid=(n // tm,),
        in_specs=[
            pl.BlockSpec((tm, d), row), pl.BlockSpec((d, wid), fixed), pl.BlockSpec((1, q_lora), fixed),
            pl.BlockSpec((1, kv_lora), fixed), pl.BlockSpec((tm, LANES), row), pl.BlockSpec((tm, LANES), row),
        ],
        out_specs=[pl.BlockSpec((tm, q_lora), row), pl.BlockSpec((tm, kv_lora), row), pl.BlockSpec((tm, kv_lora), row),
                   pl.BlockSpec((tm, LANES), row), pl.BlockSpec((tm, LANES), row)],
        out_shape=[jax.ShapeDtypeStruct((n, q_lora), BF16), jax.ShapeDtypeStruct((n, kv_lora), F32),
                   jax.ShapeDtypeStruct((n, kv_lora), BF16), jax.ShapeDtypeStruct((n, LANES), F32),
                   jax.ShapeDtypeStruct((n, LANES), BF16)],
        compiler_params=_params(("parallel",), blocks),
        name="mla_in",
    )(x, w, q_norm, kv_norm, cos, sin_signed)


def _mla_prompt_kernel(qn_ref, qr_ref, kn_ref, kr_ref, v_ref, o_ref, *, scale):
    i = pl.program_id(2)
    tq = qn_ref.shape[0]
    seq = kn_ref.shape[0]
    q = jnp.concatenate([qn_ref[...], qr_ref[...]], axis=1)
    lower = lax.broadcasted_iota(jnp.int32, (tq, tq), 1) <= lax.broadcasted_iota(jnp.int32, (tq, tq), 0)

    def scores(lo, hi):
        k = jnp.concatenate([kn_ref[lo:hi, :], kr_ref[lo:hi, :]], axis=1)
        return lax.dot_general(q, k, NT_DIMS, preferred_element_type=F32) * scale

    def attend(tile):
        past = tile * tq
        s_own = jnp.where(lower, scores(past, past + tq), MASKED)
        m = jnp.max(s_own, axis=-1, keepdims=True)
        if tile:
            s_past = scores(0, past)
            m = jnp.maximum(m, jnp.max(s_past, axis=-1, keepdims=True))
            p_past = jnp.exp(s_past - m)
        p_own = jnp.exp(s_own - m)
        l = jnp.sum(p_own, axis=-1, keepdims=True)
        o = jnp.dot(p_own.astype(BF16), v_ref[past:past + tq, :], preferred_element_type=F32)
        if tile:
            l = l + jnp.sum(p_past, axis=-1, keepdims=True)
            o = o + jnp.dot(p_past.astype(BF16), v_ref[0:past, :], preferred_element_type=F32)
        o_ref[...] = (o / l).astype(o_ref.dtype)

    for tile in range(seq // tq):
        pl.when(i == tile)(functools.partial(attend, tile))


def _mla_prompt(q2, kv, krb, batch, seq, heads, scale):
    tq = _divisor_tile(seq, 256, 16)
    nqb = seq // tq
    blocks = 2 * (3 * _nbytes((tq, LANES), BF16) + 3 * _nbytes((seq, LANES), BF16)) + 6 * _nbytes((tq, seq), F32)
    return pl.pallas_call(
        functools.partial(_mla_prompt_kernel, scale=scale),
        grid=(batch, heads, nqb),
        in_specs=[
            pl.BlockSpec((tq, LANES), lambda b, h, i: (b * nqb + i, h)),
            pl.BlockSpec((tq, LANES), lambda b, h, i: (b * nqb + i, heads + h)),
            pl.BlockSpec((seq, LANES), lambda b, h, i: (b, h)),
            pl.BlockSpec((seq, LANES), lambda b, h, i: (b, 0)),
            pl.BlockSpec((seq, LANES), lambda b, h, i: (b, heads + h)),
        ],
        out_specs=pl.BlockSpec((tq, LANES), lambda b, h, i: (b * nqb + i, h)),
        out_shape=jax.ShapeDtypeStruct((batch * seq, heads * LANES), BF16),
        compiler_params=_params(("parallel", "parallel", "parallel"), blocks),
        name="mla_prompt",
    )(q2, q2, kv, krb, kv)


def _head_mm_nt_kernel(x_ref, w_ref, o_ref):
    o_ref[...] = lax.dot_general(x_ref[...].astype(BF16), w_ref[...].astype(BF16), NT_DIMS, preferred_element_type=F32)


def _head_mm_kernel(x_ref, w_ref, o_ref):
    o_ref[...] = jnp.dot(x_ref[...].astype(BF16), w_ref[...].astype(BF16), preferred_element_type=F32).astype(o_ref.dtype)


def _mla_q_latent(q2, w_uk2d, row_block, nb, heads, kv_lora):
    blocks = 2 * (_nbytes((nb, LANES), BF16) + _nbytes((kv_lora, LANES), F32) + _nbytes((nb, kv_lora), F32)) * 2
    return pl.pallas_call(
        _head_mm_nt_kernel,
        grid=(heads,),
        in_specs=[pl.BlockSpec((nb, LANES), lambda h: (row_block, h)), pl.BlockSpec((kv_lora, LANES), lambda h: (0, h))],
        out_specs=pl.BlockSpec((nb, kv_lora), lambda h: (0, h)),
        out_shape=jax.ShapeDtypeStruct((nb, heads * kv_lora), F32),
        compiler_params=_params(("parallel",), blocks),
        name="mla_q_latent",
    )(q2, w_uk2d)


def _mla_out_latent(o_lat, w_uv2d, heads, kv_lora):
    nb = o_lat.shape[0]
    blocks = 2 * (_nbytes((nb, kv_lora), F32) + _nbytes((kv_lora, LANES), F32) + _nbytes((nb, LANES), F32)) * 2
    return pl.pallas_call(
        _head_mm_kernel,
        grid=(heads,),
        in_specs=[pl.BlockSpec((nb, kv_lora), lambda h: (0, h)), pl.BlockSpec((kv_lora, LANES), lambda h: (0, h))],
        out_specs=pl.BlockSpec((nb, LANES), lambda h: (0, h)),
        out_shape=jax.ShapeDtypeStruct((nb, heads * LANES), BF16),
        compiler_params=_params(("parallel",), blocks),
        name="mla_out_latent",
    )(o_lat, w_uv2d)


def _mla_sample_kernel(pt_ref, qlat_ref, qr_ref, cnew_ref, rnew_ref, c_hbm, r_hbm, o_ref, m_ref, l_ref, acc_ref, cbuf, rbuf, csem, rsem, *, pages_per_step, base, rope, scale):
    j = pl.program_id(1)
    stream = _PageStream(pt_ref, pages_per_step, base, ((c_hbm, cbuf, csem), (r_hbm, rbuf, rsem)))
    slot = stream.begin_step()
    qlat = qlat_ref[0]
    qr = qr_ref[0][:, :rope]

    @pl.when(j == 0)
    def _():
        c_new = cnew_ref[0]
        s_new = (jnp.sum(qlat * c_new, axis=-1, keepdims=True)
                 + jnp.sum(qr.astype(F32) * rnew_ref[0], axis=-1, keepdims=True)) * scale
        m_ref[...] = jnp.broadcast_to(s_new, m_ref.shape)
        l_ref[...] = jnp.ones_like(l_ref)
        acc_ref[...] = jnp.broadcast_to(c_new, acc_ref.shape)

    qlb = qlat.astype(BF16)
    qrb = qr.astype(BF16)
    page = cbuf.shape[2]
    cbs, parts = [], []
    for p in range(pages_per_step):
        stream.prefetch_next(p)
        cbs.append(cbuf[slot, p].astype(BF16))
        parts.append(lax.dot_general(qlb, cbs[p], NT_DIMS, preferred_element_type=F32)
                     + jnp.dot(qrb, rbuf[slot, p].astype(BF16), preferred_element_type=F32))
    s = jnp.concatenate(parts, axis=-1) * scale
    m_old = m_ref[...][:, 0:1]
    m_new = jnp.maximum(m_old, jnp.max(s, axis=-1, keepdims=True))
    corr = jnp.exp(m_old - m_new)
    e = jnp.exp(s - m_new)
    l_new = l_ref[...][:, 0:1] * corr + jnp.sum(e, axis=-1, keepdims=True)
    eb = e.astype(BF16)
    acc = acc_ref[...] * corr
    for p in range(pages_per_step):
        acc = acc + jnp.dot(eb[:, p * page:(p + 1) * page], cbs[p], preferred_element_type=F32)
    m_ref[...] = jnp.broadcast_to(m_new, m_ref.shape)
    l_ref[...] = jnp.broadcast_to(l_new, l_ref.shape)
    acc_ref[...] = acc

    @pl.when(j == pl.num_programs(1) - 1)
    def _():
        o_ref[0] = acc / l_new

    stream.end_step()


def _mla_sample(qlat, qr, c_new, r_new, lat_cache, rope_cache_t, page_table_flat, layer, n_pool, n_pages, scale):
    nb, heads, kv_lora = qlat.shape
    page = lat_cache.shape[1]
    rope = rope_cache_t.shape[1]
    pps = _divisor_tile(n_pages, 16, 1)
    per_seq = lambda b, j, pt: (b, 0, 0)
    blocks = (2 * pps * (_nbytes((page, kv_lora), F32) + _nbytes((rope, page), F32)) + 4 * _nbytes((heads, kv_lora), F32)
              + pps * _nbytes((page, kv_lora), F32))
    kern = functools.partial(_mla_sample_kernel, pages_per_step=pps, base=layer * n_pool, rope=rope, scale=scale)
    return pl.pallas_call(
        kern,
        grid_spec=pltpu.PrefetchScalarGridSpec(
            num_scalar_prefetch=1,
            grid=(nb, n_pages // pps),
            in_specs=[
                pl.BlockSpec((1, heads, kv_lora), per_seq), pl.BlockSpec((1, heads, LANES), per_seq),
                pl.BlockSpec((1, 1, kv_lora), per_seq), pl.BlockSpec((1, 1, rope), per_seq),
                pl.BlockSpec(memory_space=pl.ANY), pl.BlockSpec(memory_space=pl.ANY),
            ],
            out_specs=pl.BlockSpec((1, heads, kv_lora), per_seq),
            scratch_shapes=[pltpu.VMEM((heads, LANES), F32), pltpu.VMEM((heads, LANES), F32), pltpu.VMEM((heads, kv_lora), F32),
                            pltpu.VMEM((2, pps, page, kv_lora), F32), pltpu.VMEM((2, pps, rope, page), F32),
                            pltpu.SemaphoreType.DMA((2, pps)), pltpu.SemaphoreType.DMA((2, pps))],
        ),
        out_shape=jax.ShapeDtypeStruct((nb, heads, kv_lora), F32),
        compiler_params=_params(("arbitrary", "arbitrary"), blocks),
        name="mla_sample",
    )(page_table_flat, qlat, qr, c_new, r_new, lat_cache, rope_cache_t)


def _rope_tables(pos, dim):
    inv = ROPE_THETA ** (-jnp.arange(0, dim, 2, dtype=F32) / dim)
    ang = pos.astype(F32)[:, None] * inv[None, :]
    ang = jnp.concatenate([ang, ang], axis=-1)
    sign = jnp.where(jnp.arange(dim) < dim // 2, -1.0, 1.0).astype(F32)
    cos, sin = jnp.cos(ang), jnp.sin(ang) * sign[None, :]
    if dim < LANES:
        pad = ((0, 0), (0, LANES - dim))
        cos, sin = jnp.pad(cos, pad), jnp.pad(sin, pad)
    return cos, sin


def kernel(x_prompt, x_sample, state_conv, cache_moba_k, cache_moba_v, cache_mla_latent, cache_mla_krope, page_table, conv_w_in, conv_b_in, conv_w_dw, conv_b_dw, conv_ln_g, conv_ln_b, conv_w_out, conv_b_out, moba_w_qkv, moba_w_o, mla_w_in, mla_q_norm, mla_w_uq, mla_kv_norm, mla_w_uk, mla_w_uv, mla_w_o, mlp_w1, mlp_w2, ln_g, ln_b):
    bp, sp, d = x_prompt.shape
    bs, ts, _ = x_sample.shape
    assert ts == 1 and (bp * sp) % bs == 0
    n_prompt = bp * sp
    depth = mlp_w1.shape[0]
    alpha = (2 * depth) ** 0.25
    n_pages = page_table.shape[1]
    n_pool, page, n_kv, hd = cache_moba_k.shape[1:]
    past_len = n_pages * page
    assert past_len % MOBA_BLOCK == 0
    n_q = moba_w_qkv.shape[2] // hd - 2 * n_kv
    q_lora = mla_q_norm.shape[1]
    kv_lora, heads, nope = mla_w_uk.shape[1:]
    v_dim = mla_w_uv.shape[3]
    rope = cache_mla_krope.shape[3]
    conv_width = conv_w_dw.shape[1]
    assert nope == LANES and v_dim == LANES and rope <= LANES and sp >= conv_width - 1
    mla_scale = (nope + rope) ** -0.5

    pos = jnp.concatenate([jnp.tile(jnp.arange(sp, dtype=jnp.int32), bp), jnp.full((bs,), past_len, jnp.int32)])
    cos_moba, sin_moba = _rope_tables(pos, hd)
    cos_mla, sin_mla = _rope_tables(pos, rope)
    page_table_flat = page_table.reshape(-1).astype(jnp.int32)
    row1 = lambda v: v.reshape(1, -1)

    h = jnp.concatenate([x_prompt.reshape(n_prompt, d), x_sample.reshape(bs, d)], axis=0)
    hb = h.astype(BF16)
    mlp_w1_b, mlp_w2_b = mlp_w1.astype(BF16), mlp_w2.astype(BF16)
    conv_w_in_b, conv_w_out_b = conv_w_in.astype(BF16), conv_w_out.astype(BF16)
    moba_w_o_b, mla_w_o_b = moba_w_o.astype(BF16), mla_w_o.astype(BF16)
    no_bias = jnp.zeros((1, d), F32)

    conv_p, conv_s = [], []
    mk_p, mv_p, mk_s, mv_s = [], [], [], []
    ml_p, mr_p, ml_s, mr_s = [], [], [], []
    for i in range(depth):
        kind, j = i % 3, i // 3
        if kind == 0:
            u = _glu_matmul(hb, conv_w_in_b, j, row1(conv_b_in[j]))
            mixed_p = _conv_prompt(u, conv_w_dw[j], row1(conv_b_dw[j]), row1(conv_ln_g[j]), row1(conv_ln_b[j]), bp, sp)
            prev = state_conv[j].astype(F32)
            mixed_s = _conv_sample(prev.transpose(1, 0, 2), u, n_prompt // bs, conv_w_dw[j], row1(conv_b_dw[j]),
                                   row1(conv_ln_g[j]), row1(conv_ln_b[j]))
            conv_p.append(jnp.stack([u[(b + 1) * sp - (conv_width - 1):(b + 1) * sp] for b in range(bp)]))
            conv_s.append(jnp.concatenate([prev[:, 1:], u[n_prompt:].reshape(bs, 1, -1)], axis=1))
            mixed = jnp.concatenate([mixed_p, mixed_s], axis=0)
            w_out, b_out = conv_w_out_b, row1(conv_b_out[j])
        elif kind == 1:
            nq_cols, nk_cols = n_q * hd, n_kv * hd
            qkv = _matmul_rope(hb, moba_w_qkv[j].astype(BF16), cos_moba, sin_moba, F32, (0, nq_cols + nk_cols), hd // 2,
                               tn=math.gcd(nq_cols, nk_cols))
            mixed_p = _moba_prompt(qkv, bp, sp, n_q, n_kv, hd)
            group = n_q // n_kv
            q_s = qkv[n_prompt:, :nq_cols].reshape(bs, n_kv, group, hd).transpose(0, 2, 1, 3).reshape(bs, n_q, hd)
            k_s = qkv[n_prompt:, nq_cols:nq_cols + nk_cols].reshape(bs, n_kv, hd)
            v_s = qkv[n_prompt:, nq_cols + nk_cols:].reshape(bs, n_kv, hd)
            o_s = _moba_sample(q_s, k_s, v_s,
                               cache_moba_k.reshape(-1, page * n_kv, hd), cache_moba_v.reshape(-1, page * n_kv, hd),
                               page_table_flat, j, n_pool, n_pages, page, n_kv)
            mixed_s = o_s.reshape(bs, group, n_kv, hd).transpose(0, 2, 1, 3).reshape(bs, nq_cols).astype(BF16)
            mk_p.append(qkv[:n_prompt, nq_cols:nq_cols + nk_cols].reshape(bp, sp, n_kv, hd))
            mv_p.append(qkv[:n_prompt, nq_cols + nk_cols:].reshape(bp, sp, n_kv, hd))
            mk_s.append(k_s.reshape(bs, 1, n_kv, hd))
            mv_s.append(v_s.reshape(bs, 1, n_kv, hd))
            mixed = jnp.concatenate([mixed_p, mixed_s], axis=0)
            w_out, b_out = moba_w_o_b, no_bias
        else:
            w_in = jnp.pad(mla_w_in[j], ((0, 0), (0, LANES - rope))).astype(BF16)
            cq, ckv, ckv_b, kr, kr_b = _mla_in(hb, w_in, row1(mla_q_norm[j]), row1(mla_kv_norm[j]), cos_mla, sin_mla,
                                               q_lora, kv_lora, rope // 2)
            w_uq = mla_w_uq[j].reshape(q_lora, heads, nope + rope)
            w_uq_rope = jnp.pad(w_uq[:, :, nope:], ((0, 0), (0, 0), (0, LANES - rope)))
            w_uq2 = jnp.concatenate([w_uq[:, :, :nope].reshape(q_lora, heads * nope),
                                     w_uq_rope.reshape(q_lora, heads * LANES)], axis=1).astype(BF16)
            q2 = _matmul_rope(cq, w_uq2, cos_mla, sin_mla, BF16, (heads * nope, heads * (nope + LANES)), rope // 2,
                              tn=_divisor_tile(heads * LANES, 512, LANES))
            w_uk2d = mla_w_uk[j].reshape(kv_lora, heads * nope)
            w_uv2d = mla_w_uv[j].reshape(kv_lora, heads * v_dim)
            kv = _matmul(ckv_b, jnp.concatenate([w_uk2d, w_uv2d], axis=1).astype(BF16), BF16, rows=n_prompt)
            mixed_p = _mla_prompt(q2, kv, kr_b, bp, sp, heads, mla_scale)
            qlat = _mla_q_latent(q2, w_uk2d, n_prompt // bs, bs, heads, kv_lora).reshape(bs, heads, kv_lora)
            qr_s = q2[n_prompt:, heads * nope:].reshape(bs, heads, LANES)
            c_new = ckv[n_prompt:].reshape(bs, 1, kv_lora)
            r_new = kr[n_prompt:, :rope].reshape(bs, 1, rope)
            o_lat = _mla_sample(qlat, qr_s, c_new, r_new,
                                cache_mla_latent.reshape(-1, page, kv_lora),
                                cache_mla_krope.transpose(0, 1, 3, 2).reshape(-1, rope, page),
                                page_table_flat, j, cache_mla_latent.shape[1], n_pages, mla_scale)
            mixed_s = _mla_out_latent(o_lat.reshape(bs, heads * kv_lora), w_uv2d, heads, kv_lora)
            ml_p.append(ckv[:n_prompt].reshape(bp, sp, kv_lora))
            mr_p.append(kr[:n_prompt, :rope].reshape(bp, sp, rope))
            ml_s.append(c_new)
            mr_s.append(r_new)
            mixed = jnp.concatenate([mixed_p, mixed_s], axis=0)
            w_out, b_out = mla_w_o_b, no_bias
        h, hb = _matmul_res_ln(mixed, w_out, j, b_out, h, row1(ln_g[i, 0]), row1(ln_b[i, 0]), alpha)
        h, hb = _mlp_res_ln(hb, h, mlp_w1_b, mlp_w2_b, i, row1(ln_g[i, 1]), row1(ln_b[i, 1]), alpha)

    return (h[:n_prompt].reshape(bp, sp, d), h[n_prompt:].reshape(bs, 1, d),
            jnp.stack(conv_p), jnp.stack(conv_s),
            jnp.stack(mk_p), jnp.stack(mv_p), jnp.stack(mk_s), jnp.stack(mv_s),
            jnp.stack(ml_p), jnp.stack(mr_p), jnp.stack(ml_s), jnp.stack(mr_s))
```

```python
import functools
import math

import numpy as np
import jax
import jax.numpy as jnp
from jax import lax
from jax.experimental import pallas as pl
from jax.experimental.pallas import tpu as pltpu

F32 = jnp.float32
BF16 = jnp.bfloat16

MOBA_BLOCK = 256
MOBA_TOPK = 3
ROPE_THETA = 10000.0
LN_EPS = 1e-5
RMS_EPS = 1e-6
MASKED = -1e30

LANES = 128
SUBLANES = 8
VMEM_CAP_BYTES = 60 * 1024 * 1024
VMEM_FLOOR_BYTES = 32 * 1024 * 1024

NT_DIMS = (((1,), (1,)), ((), ()))


def _divisor_tile(n, target, mult):
    best = None
    for t in range(mult, min(n, target) + 1, mult):
        if n % t == 0:
            best = t
    return n if best is None else best


def _params(semantics, block_bytes):
    limit = int(min(max(block_bytes * 1.3 + (6 << 20), VMEM_FLOOR_BYTES), VMEM_CAP_BYTES))
    return pltpu.CompilerParams(dimension_semantics=semantics, vmem_limit_bytes=limit)


def _nbytes(shape, dtype):
    return int(np.prod(shape)) * jnp.dtype(dtype).itemsize


def _layer_norm(z, g, b):
    mu = jnp.mean(z, axis=-1, keepdims=True)
    zc = z - mu
    var = jnp.mean(zc * zc, axis=-1, keepdims=True)
    return zc * lax.rsqrt(var + LN_EPS) * g + b


def _sigmoid(x):
    return 1.0 / (1.0 + jnp.exp(-x))


PAGE_SLOTS = 3


class _PageStream:
    def __init__(self, pt_ref, pages_per_step, base, streams):
        steps = pl.num_programs(1)
        self.pt_ref, self.pages, self.base, self.streams = pt_ref, pages_per_step, base, streams
        self.step = pl.program_id(0) * steps + pl.program_id(1)
        self.last = pl.num_programs(0) * steps - 1

    def _copies(self, wave, p):
        page = self.base + self.pt_ref[jnp.minimum(wave, self.last) * self.pages + p]
        slot = lax.rem(wave, PAGE_SLOTS)
        return [pltpu.make_async_copy(hbm.at[page], buf.at[slot, p], sem.at[slot, p]) for hbm, buf, sem in self.streams]

    def _all(self, wave, op):
        for p in range(self.pages):
            for copy in self._copies(wave, p):
                op(copy)

    def begin_step(self):
        @pl.when(self.step == 0)
        def _():
            self._all(self.step, lambda copy: copy.start())
            self._all(self.step + 1, lambda copy: copy.start())

        self._all(self.step, lambda copy: copy.wait())
        return lax.rem(self.step, PAGE_SLOTS)

    def prefetch(self, p):
        for copy in self._copies(self.step + 2, p):
            copy.start()

    def end_step(self):
        @pl.when(self.step == self.last)
        def _():
            self._all(self.step + 1, lambda copy: copy.wait())
            self._all(self.step + 2, lambda copy: copy.wait())


def _mm_kernel(x_ref, w_ref, o_ref):
    o_ref[...] = jnp.dot(x_ref[...], w_ref[...], preferred_element_type=F32).astype(o_ref.dtype)


def _matmul(x, w, out_dtype, rows=None, tm_target=1024, tn_target=512):
    rows = x.shape[0] if rows is None else rows
    k, nout = w.shape
    tm = _divisor_tile(rows, tm_target, 16)
    tn = _divisor_tile(nout, tn_target, LANES)
    blocks = 2 * (_nbytes((tm, k), BF16) + _nbytes((k, tn), BF16) + _nbytes((tm, tn), out_dtype)) + _nbytes((tm, tn), F32)
    return pl.pallas_call(
        _mm_kernel,
        grid=(rows // tm, nout // tn),
        in_specs=[pl.BlockSpec((tm, k), lambda i, j: (i, 0)), pl.BlockSpec((k, tn), lambda i, j: (0, j))],
        out_specs=pl.BlockSpec((tm, tn), lambda i, j: (i, j)),
        out_shape=jax.ShapeDtypeStruct((rows, nout), out_dtype),
        compiler_params=_params(("parallel", "parallel"), blocks),
        name="matmul",
    )(x, w)


def _glu_kernel(x_ref, wa_ref, wg_ref, ba_ref, bg_ref, o_ref):
    x = x_ref[...]
    a = jnp.dot(x, wa_ref[...], preferred_element_type=F32) + ba_ref[...]
    g = jnp.dot(x, wg_ref[...], preferred_element_type=F32) + bg_ref[...]
    o_ref[...] = a * _sigmoid(g)


def _glu_matmul(x, w, layer, b):
    n, k = x.shape
    c = w.shape[2] // 2
    tm = _divisor_tile(n, 1024, 16)
    tn = _divisor_tile(c, 512, LANES)
    nj = c // tn
    blocks = 2 * (_nbytes((tm, k), BF16) + 2 * _nbytes((k, tn), BF16) + _nbytes((tm, tn), F32)) + 3 * _nbytes((tm, tn), F32)
    return pl.pallas_call(
        _glu_kernel,
        grid=(n // tm, nj),
        in_specs=[
            pl.BlockSpec((tm, k), lambda i, j: (i, 0)),
            pl.BlockSpec((None, k, tn), lambda i, j: (layer, 0, j)),
            pl.BlockSpec((None, k, tn), lambda i, j: (layer, 0, nj + j)),
            pl.BlockSpec((1, tn), lambda i, j: (0, j)),
            pl.BlockSpec((1, tn), lambda i, j: (0, nj + j)),
        ],
        out_specs=pl.BlockSpec((tm, tn), lambda i, j: (i, j)),
        out_shape=jax.ShapeDtypeStruct((n, c), F32),
        compiler_params=_params(("parallel", "parallel"), blocks),
        name="glu_matmul",
    )(x, w, w, b, b)


def _rotate_half(a, half):
    if 2 * half == LANES:
        return pltpu.roll(a, half, 1)
    lane = lax.broadcasted_iota(jnp.int32, a.shape, 1)
    return jnp.where(lane % (2 * half) < half, pltpu.roll(a, LANES - half, 1), pltpu.roll(a, half, 1))


def _mm_rope_kernel(x_ref, w_ref, cos_ref, sin_ref, o_ref, *, rope_lo, rope_hi, half):
    acc = jnp.dot(x_ref[...], w_ref[...], preferred_element_type=F32)
    j = pl.program_id(1)
    in_rope = jnp.logical_and(j >= rope_lo, j < rope_hi)

    @pl.when(in_rope)
    def _():
        cos = cos_ref[...]
        sin = sin_ref[...]
        for c0 in range(0, acc.shape[1], LANES):
            a = acc[:, c0:c0 + LANES]
            o_ref[:, c0:c0 + LANES] = (a * cos + _rotate_half(a, half) * sin).astype(o_ref.dtype)

    @pl.when(jnp.logical_not(in_rope))
    def _():
        o_ref[...] = acc.astype(o_ref.dtype)


def _matmul_rope(x, w, cos, sin_signed, out_dtype, rope_cols, half, tn):
    n, k = x.shape
    nout = w.shape[1]
    tm = _divisor_tile(n, 1024, 16)
    assert nout % tn == 0 and rope_cols[0] % tn == 0 and rope_cols[1] % tn == 0
    blocks = 2 * (_nbytes((tm, k), BF16) + _nbytes((k, tn), BF16) + 2 * _nbytes((tm, LANES), F32)
                  + _nbytes((tm, tn), out_dtype)) + 2 * _nbytes((tm, tn), F32)
    kern = functools.partial(_mm_rope_kernel, rope_lo=rope_cols[0] // tn, rope_hi=rope_cols[1] // tn, half=half)
    return pl.pallas_call(
        kern,
        grid=(n // tm, nout // tn),
        in_specs=[
            pl.BlockSpec((tm, k), lambda i, j: (i, 0)),
            pl.BlockSpec((k, tn), lambda i, j: (0, j)),
            pl.BlockSpec((tm, LANES), lambda i, j: (i, 0)),
            pl.BlockSpec((tm, LANES), lambda i, j: (i, 0)),
        ],
        out_specs=pl.BlockSpec((tm, tn), lambda i, j: (i, j)),
        out_shape=jax.ShapeDtypeStruct((n, nout), out_dtype),
        compiler_params=_params(("parallel", "parallel"), blocks),
        name="matmul_rope",
    )(x, w, cos, sin_signed)


def _mm_res_ln_kernel(x_ref, w_ref, bias_ref, res_ref, g_ref, b_ref, of_ref, ob_ref, *, alpha):
    y = jnp.dot(x_ref[...], w_ref[...], preferred_element_type=F32) + bias_ref[...]
    out = _layer_norm(alpha * res_ref[...] + y, g_ref[...], b_ref[...])
    of_ref[...] = out
    ob_ref[...] = out.astype(BF16)


def _matmul_res_ln(x, w, layer, bias, res, g, b, alpha):
    n, k = x.shape
    d = w.shape[2]
    tm = _divisor_tile(n, 640, 16)
    blocks = (2 * (_nbytes((tm, k), BF16) + _nbytes((k, d), BF16) + 2 * _nbytes((tm, d), F32) + _nbytes((tm, d), BF16))
              + 3 * _nbytes((tm, d), F32))
    row = lambda i: (i, 0)
    fixed = lambda i: (0, 0)
    return pl.pallas_call(
        functools.partial(_mm_res_ln_kernel, alpha=alpha),
        grid=(n // tm,),
        in_specs=[
            pl.BlockSpec((tm, k), row), pl.BlockSpec((None, k, d), lambda i: (layer, 0, 0)), pl.BlockSpec((1, d), fixed),
            pl.BlockSpec((tm, d), row), pl.BlockSpec((1, d), fixed), pl.BlockSpec((1, d), fixed),
        ],
        out_specs=[pl.BlockSpec((tm, d), row), pl.BlockSpec((tm, d), row)],
        out_shape=[jax.ShapeDtypeStruct((n, d), F32), jax.ShapeDtypeStruct((n, d), BF16)],
        compiler_params=_params(("parallel",), blocks),
        name="matmul_res_ln",
    )(x, w, bias, res, g, b)


def _mlp_kernel(xb_ref, xf_ref, w1_ref, w2_ref, g_ref, b_ref, of_ref, ob_ref, acc_ref, *, alpha):
    f = pl.program_id(1)

    @pl.when(f == 0)
    def _():
        acc_ref[...] = jnp.zeros_like(acc_ref)

    h1 = jnp.maximum(jnp.dot(xb_ref[...], w1_ref[...], preferred_element_type=F32), 0.0)
    acc_ref[...] += jnp.dot((h1 * h1).astype(BF16), w2_ref[...], preferred_element_type=F32)

    @pl.when(f == pl.num_programs(1) - 1)
    def _():
        out = _layer_norm(alpha * xf_ref[...] + acc_ref[...], g_ref[...], b_ref[...])
        of_ref[...] = out
        ob_ref[...] = out.astype(BF16)


def _mlp_res_ln(xb, xf, w1, w2, layer, g, b, alpha):
    n, d = xb.shape
    ff = w1.shape[2]
    tm = _divisor_tile(n, 640, 16)
    tf = _divisor_tile(ff, 512, LANES)
    blocks = (2 * (_nbytes((tm, d), BF16) * 2 + 2 * _nbytes((tm, d), F32) + _nbytes((d, tf), BF16) + _nbytes((tf, d), BF16))
              + _nbytes((tm, d), F32) + 2 * _nbytes((tm, tf), F32))
    row = lambda i, f: (i, 0)
    fixed = lambda i, f: (0, 0)
    return pl.pallas_call(
        functools.partial(_mlp_kernel, alpha=alpha),
        grid=(n // tm, ff // tf),
        in_specs=[
            pl.BlockSpec((tm, d), row), pl.BlockSpec((tm, d), row),
            pl.BlockSpec((None, d, tf), lambda i, f: (layer, 0, f)), pl.BlockSpec((None, tf, d), lambda i, f: (layer, f, 0)),
            pl.BlockSpec((1, d), fixed), pl.BlockSpec((1, d), fixed),
        ],
        out_specs=[pl.BlockSpec((tm, d), row), pl.BlockSpec((tm, d), row)],
        out_shape=[jax.ShapeDtypeStruct((n, d), F32), jax.ShapeDtypeStruct((n, d), BF16)],
        scratch_shapes=[pltpu.VMEM((tm, d), F32)],
        compiler_params=_params(("parallel", "arbitrary"), blocks),
        name="mlp_res_ln",
    )(xb, xf, w1, w2, g, b)


CONV_HALO = 32
CONV_ROWS = 128
CONV_COLS = 128


def _conv_prompt_kernel(halo_ref, u_ref, w_ref, bdw_ref, g_ref, b_ref, o_ref, win_ref, sh_ref, c_ref, *, width, tt):
    first = pl.program_id(1) == 0
    win_ref[0:CONV_HALO, :] = jnp.where(first, 0.0, halo_ref[...])
    win_ref[CONV_HALO:CONV_HALO + tt, :] = u_ref[...]
    chans = u_ref.shape[1]
    off0 = CONV_HALO - (width - 1)
    nrows = sh_ref.shape[1]
    for s in range(1, SUBLANES):
        sh_ref[s - 1] = win_ref[s:s + nrows, :]
    def chunk(ci, carry):
        cs = pl.ds(pl.multiple_of(ci * CONV_COLS, CONV_COLS), CONV_COLS)
        for r0 in range(0, tt, CONV_ROWS):
            acc = jnp.zeros((CONV_ROWS, CONV_COLS), F32)
            for k in range(width):
                s = (off0 + k) % SUBLANES
                row = r0 + off0 + k - s
                if s:
                    x = sh_ref[s - 1, row:row + CONV_ROWS, cs]
                else:
                    x = win_ref[row:row + CONV_ROWS, cs]
                acc = acc + w_ref[k:k + 1, cs] * x
            c_ref[r0:r0 + CONV_ROWS, cs] = acc + bdw_ref[:, cs]
        return carry

    lax.fori_loop(0, chans // CONV_COLS, chunk, 0)
    y = _layer_norm(c_ref[...], g_ref[...], b_ref[...])
    o_ref[...] = (y * _sigmoid(y)).astype(o_ref.dtype)


def _conv_prompt(u, w_dw, b_dw, g, b, batch, seq):
    chans = u.shape[1]
    width = w_dw.shape[0]
    tt = _divisor_tile(seq, 128, CONV_ROWS)
    assert width - 1 <= CONV_HALO and tt % CONV_HALO == 0 and tt % CONV_ROWS == 0 and chans % CONV_COLS == 0
    nt = seq // tt
    per = tt // CONV_HALO
    shift_rows = tt + (CONV_HALO - SUBLANES)
    blocks = (2 * (_nbytes((CONV_HALO, chans), F32) + _nbytes((tt, chans), F32) + _nbytes((width, chans), F32)
                   + _nbytes((tt, chans), BF16)) + _nbytes((tt + CONV_HALO, chans), F32) + 3 * _nbytes((tt, chans), F32)
              + (SUBLANES - 1) * _nbytes((shift_rows, chans), F32))
    fixed = lambda bi, i: (0, 0)
    return pl.pallas_call(
        functools.partial(_conv_prompt_kernel, width=width, tt=tt),
        grid=(batch, nt),
        in_specs=[
            pl.BlockSpec((CONV_HALO, chans), lambda bi, i: (jnp.maximum((bi * nt + i) * per - 1, 0), 0)),
            pl.BlockSpec((tt, chans), lambda bi, i: (bi * nt + i, 0)),
            pl.BlockSpec((width, chans), fixed), pl.BlockSpec((1, chans), fixed),
            pl.BlockSpec((1, chans), fixed), pl.BlockSpec((1, chans), fixed),
        ],
        out_specs=pl.BlockSpec((tt, chans), lambda bi, i: (bi * nt + i, 0)),
        out_shape=jax.ShapeDtypeStruct((batch * seq, chans), BF16),
        scratch_shapes=[pltpu.VMEM((tt + CONV_HALO, chans), F32), pltpu.VMEM((SUBLANES - 1, shift_rows, chans), F32),
                        pltpu.VMEM((tt, chans), F32)],
        compiler_params=_params(("parallel", "arbitrary"), blocks),
        name="conv_prompt",
    )(u, u, w_dw, b_dw, g, b)


def _conv_sample_kernel(prev_ref, u_ref, w_ref, bdw_ref, g_ref, b_ref, o_ref):
    taps = prev_ref.shape[0]
    c = u_ref[...] * w_ref[taps:taps + 1, :] + bdw_ref[...]
    for k in range(taps):
        c = c + prev_ref[k] * w_ref[k:k + 1, :]
    y = _layer_norm(c, g_ref[...], b_ref[...])
    o_ref[...] = (y * _sigmoid(y)).astype(o_ref.dtype)


def _conv_sample(prev_t, u, row_block, w_dw, b_dw, g, b):
    taps, nb, chans = prev_t.shape
    tb = _divisor_tile(nb, 16, 16)
    per = nb // tb
    blocks = 2 * (_nbytes((taps, tb, chans), F32) + _nbytes((taps + 1, chans), F32) + 2 * _nbytes((tb, chans), F32)) + 3 * _nbytes((tb, chans), F32)
    fixed = lambda i: (0, 0)
    return pl.pallas_call(
        _conv_sample_kernel,
        grid=(per,),
        in_specs=[
            pl.BlockSpec((taps, tb, chans), lambda i: (0, i, 0)),
            pl.BlockSpec((tb, chans), lambda i: (row_block * per + i, 0)),
            pl.BlockSpec((taps + 1, chans), fixed), pl.BlockSpec((1, chans), fixed),
            pl.BlockSpec((1, chans), fixed), pl.BlockSpec((1, chans), fixed),
        ],
        out_specs=pl.BlockSpec((tb, chans), lambda i: (i, 0)),
        out_shape=jax.ShapeDtypeStruct((nb, chans), BF16),
        compiler_params=_params(("parallel",), blocks),
        name="conv_sample",
    )(prev_t, u, w_dw, b_dw, g, b)


def _moba_prompt_kernel(q_ref, k_ref, v_ref, e_ref, nidx_ref, o_ref, kb_ref, vb_ref, mean_ref, *, group, hd, blk, nbk, topk, scale):
    i = pl.program_id(2)

    @pl.when(i == 0)
    def _():
        k = k_ref[...]
        kb_ref[...] = k.astype(BF16)
        vb_ref[...] = v_ref[...].astype(BF16)
        mean_ref[...] = jnp.zeros_like(mean_ref)
        for n in range(nbk):
            mean_n = jnp.sum(k[n * blk:(n + 1) * blk], axis=0, keepdims=True) / blk
            for g in range(group):
                mean_ref[g * nbk + n:g * nbk + n + 1, g * hd:(g + 1) * hd] = mean_n

    q = q_ref[...]
    gate = lax.dot_general(q, mean_ref[...], NT_DIMS, precision=lax.Precision.HIGHEST, preferred_element_type=F32)
    nidx = nidx_ref[...]
    past = nidx < i
    gm = jnp.where(past, gate, -jnp.inf)
    rank = jnp.zeros(gm.shape, F32)
    for r in range(1, nbk):
        same_side = nidx >= r
        other = jnp.where(same_side, pltpu.roll(gm, r, 1), pltpu.roll(gm, (r - nbk) % LANES, 1))
        beats = jnp.logical_or(other > gm, jnp.logical_and(other == gm, same_side))
        rank = rank + jnp.where(beats, 1.0, 0.0)
    selected = jnp.logical_and(past, rank < topk)
    sel_bias = jnp.where(selected, 0.0, MASKED).astype(BF16)
    qb = q.astype(BF16)
    lower = lax.broadcasted_iota(jnp.int32, (blk, blk), 1) <= lax.broadcasted_iota(jnp.int32, (blk, blk), 0)

    def attend(own):
        past_keys = own * blk
        k_own = kb_ref[past_keys:past_keys + blk, :]
        v_own = vb_ref[past_keys:past_keys + blk, :]
        for g in range(group):
            qg = qb[:, g * hd:(g + 1) * hd]
            s_own = jnp.where(lower, lax.dot_general(qg, k_own, NT_DIMS, preferred_element_type=F32) * scale, MASKED)
            m = jnp.max(s_own, axis=-1, keepdims=True)
            if own:
                bias = jnp.dot(sel_bias, e_ref[g, :, 0:past_keys], preferred_element_type=F32)
                s_past = lax.dot_general(qg, kb_ref[0:past_keys, :], NT_DIMS, preferred_element_type=F32) * scale + bias
                m = jnp.maximum(m, jnp.max(s_past, axis=-1, keepdims=True))
                p_past = jnp.exp(s_past - m)
            p_own = jnp.exp(s_own - m)
            l = jnp.sum(p_own, axis=-1, keepdims=True)
            o = jnp.dot(p_own.astype(BF16), v_own, preferred_element_type=F32)
            if own:
                l = l + jnp.sum(p_past, axis=-1, keepdims=True)
                o = o + jnp.dot(p_past.astype(BF16), vb_ref[0:past_keys, :], preferred_element_type=F32)
            o_ref[:, g * hd:(g + 1) * hd] = (o / l).astype(o_ref.dtype)

    for own in range(nbk):
        pl.when(i == own)(functools.partial(attend, own))


def _moba_prompt(qkv, batch, seq, n_q, n_kv, hd):
    blk = MOBA_BLOCK
    assert seq % blk == 0 and hd == LANES
    nbk = seq // blk
    group = n_q // n_kv
    assert group * nbk <= LANES
    topk = min(MOBA_TOPK, nbk)
    lane = np.arange(LANES)
    nidx = np.where(lane < group * nbk, lane % nbk, 1 << 20).astype(np.int32)[None, :]
    spread = np.zeros((group, LANES, seq), np.float32)
    for g in range(group):
        for n in range(nbk):
            spread[g, g * nbk + n, n * blk:(n + 1) * blk] = 1.0
    nqb = seq // blk
    blocks = (2 * (_nbytes((blk, group * hd), F32) + 2 * _nbytes((seq, hd), F32) + _nbytes((group, LANES, seq), BF16)
                   + _nbytes((blk, group * hd), BF16)) + 2 * _nbytes((seq, hd), BF16) + _nbytes((LANES, group * hd), F32)
              + 6 * _nbytes((blk, seq), F32))
    kern = functools.partial(_moba_prompt_kernel, group=group, hd=hd, blk=blk, nbk=nbk, topk=topk, scale=hd ** -0.5)
    return pl.pallas_call(
        kern,
        grid=(batch, n_kv, nqb),
        in_specs=[
            pl.BlockSpec((blk, group * hd), lambda b, h, i: (b * nqb + i, h)),
            pl.BlockSpec((seq, hd), lambda b, h, i: (b, n_q + h)),
            pl.BlockSpec((seq, hd), lambda b, h, i: (b, n_q + n_kv + h)),
            pl.BlockSpec((group, LANES, seq), lambda b, h, i: (0, 0, 0)),
            pl.BlockSpec((1, LANES), lambda b, h, i: (0, 0)),
        ],
        out_specs=pl.BlockSpec((blk, group * hd), lambda b, h, i: (b * nqb + i, h)),
        out_shape=jax.ShapeDtypeStruct((batch * seq, n_q * hd), BF16),
        scratch_shapes=[pltpu.VMEM((seq, hd), BF16), pltpu.VMEM((seq, hd), BF16), pltpu.VMEM((LANES, group * hd), F32)],
        compiler_params=_params(("parallel", "parallel", "arbitrary"), blocks),
        name="moba_prompt",
    )(qkv, qkv, qkv, jnp.asarray(spread, BF16), jnp.asarray(nidx))


def _rowsum8(x):
    parts = x.reshape(x.shape[0] // SUBLANES, SUBLANES, x.shape[1])
    while parts.shape[0] > 1 and parts.shape[0] % 2 == 0:
        half = parts.shape[0] // 2
        parts = parts[:half] + parts[half:]
    return jnp.sum(parts, axis=0)


def _moba_sample_kernel(pt_ref, q_ref, knew_ref, vnew_ref, mask_ref, lane_blk_ref, lane_kv_ref, k_hbm, v_hbm, o_ref, m_ref, l_ref, acc_ref, ksum_ref, mean_ref, kbuf, vbuf, ksem, vsem, *, pages_per_step, base, ppb, nb, group, n_kv, hd, topk, scale):
    j = pl.program_id(1)
    stream = _PageStream(pt_ref, pages_per_step, base, ((k_hbm, kbuf, ksem), (v_hbm, vbuf, vsem)))
    slot = stream.begin_step()
    q = q_ref[0]
    qb = q.astype(BF16)
    mask = mask_ref[...]
    n_q = q.shape[0]
    lane_blk = lane_blk_ref[...]
    bps = pages_per_step // ppb

    @pl.when(j == 0)
    def _():
        m_ref[...] = jnp.zeros_like(m_ref)
        l_ref[...] = jnp.zeros_like(l_ref)

    kps, scores = [], []
    for p in range(pages_per_step):
        stream.prefetch(p)
        kps.append(kbuf[slot, p])
        scores.append(lax.dot_general(qb, kps[p].astype(BF16), NT_DIMS, preferred_element_type=F32))
    m_all = m_ref[...]
    l_all = l_ref[...]
    for bi in range(bps):
        blk = j * bps + bi
        s = jnp.concatenate(scores[bi * ppb:(bi + 1) * ppb], axis=-1) * scale + mask
        m = jnp.max(s, axis=-1, keepdims=True)
        e = jnp.exp(s - m)
        l = jnp.sum(e, axis=-1, keepdims=True)
        eb = e.astype(BF16)
        cols = s.shape[1] // ppb
        acc = jnp.dot(eb[:, 0:cols], vbuf[slot, bi * ppb].astype(BF16), preferred_element_type=F32)
        ksum = _rowsum8(kps[bi * ppb])
        for t in range(1, ppb):
            acc = acc + jnp.dot(eb[:, t * cols:(t + 1) * cols], vbuf[slot, bi * ppb + t].astype(BF16), preferred_element_type=F32)
            ksum = ksum + _rowsum8(kps[bi * ppb + t])
        hit = lane_blk == blk
        m_all = jnp.where(hit, m, m_all)
        l_all = jnp.where(hit, l, l_all)
        acc_ref[blk] = acc
        ksum_ref[blk] = ksum
    m_ref[...] = m_all
    l_ref[...] = l_all

    @pl.when(j == pl.num_programs(1) - 1)
    def _():
        mean_ref[...] = jnp.zeros_like(mean_ref)
        for n in range(nb):
            bs = ksum_ref[n]
            per_head = bs[0:n_kv]
            for t in range(1, SUBLANES // n_kv):
                per_head = per_head + bs[t * n_kv:(t + 1) * n_kv]
            mean_ref[n * n_kv:(n + 1) * n_kv, :] = per_head / MOBA_BLOCK
        gate = lax.dot_general(q, mean_ref[...], NT_DIMS, precision=lax.Precision.HIGHEST, preferred_element_type=F32)
        row_kv = lax.broadcasted_iota(jnp.int32, gate.shape, 0) % n_kv
        valid = jnp.logical_and(lane_kv_ref[...] == row_kv, lane_blk < nb)
        gm = jnp.where(valid, gate, -jnp.inf)
        rank = jnp.zeros(gm.shape, F32)
        for r in range(1, nb):
            same_side = lane_blk >= r
            other = jnp.where(same_side, pltpu.roll(gm, r * n_kv, 1), pltpu.roll(gm, (r * n_kv - nb * n_kv) % LANES, 1))
            beats = jnp.logical_or(other > gm, jnp.logical_and(other == gm, same_side))
            rank = rank + jnp.where(beats, 1.0, 0.0)
        sel = jnp.logical_and(valid, rank < topk)

        k_new = jnp.concatenate([knew_ref[0]] * group, axis=0)
        v_new = jnp.concatenate([vnew_ref[0]] * group, axis=0)
        s_new = jnp.sum(q * k_new, axis=-1, keepdims=True) * scale
        m_tot = jnp.maximum(s_new, jnp.max(jnp.where(sel, m_all, MASKED), axis=-1, keepdims=True))
        w_all = jnp.where(sel, jnp.exp(m_all - m_tot), 0.0)
        w_new = jnp.exp(s_new - m_tot)
        l_tot = w_new + jnp.sum(w_all * l_all, axis=-1, keepdims=True)
        o_tot = w_new * v_new
        for n in range(nb):
            w_n = jnp.sum(jnp.where(lane_blk == n, w_all, 0.0), axis=-1, keepdims=True)
            o_tot = o_tot + w_n * acc_ref[n]
        o_ref[0] = o_tot / l_tot

    stream.end_step()


def _moba_sample(q, k_new, v_new, cache_k, cache_v, page_table_flat, layer, n_pool, n_pages, page, n_kv):
    nseq, n_q, hd = q.shape
    group = n_q // n_kv
    ppb = MOBA_BLOCK // page
    assert MOBA_BLOCK % page == 0 and n_pages % ppb == 0 and SUBLANES % n_kv == 0 and hd == LANES
    nb = n_pages // ppb
    assert nb * n_kv <= LANES
    pps = _divisor_tile(n_pages, 16, ppb)
    topk = min(MOBA_TOPK, nb + 1)
    rows = np.arange(n_q)[:, None] % n_kv
    cols = np.arange(ppb * page * n_kv)[None, :] % n_kv
    mask = jnp.asarray(np.where(rows == cols, 0.0, MASKED).astype(np.float32))
    lane = np.arange(LANES)
    lane_blk = jnp.asarray((lane // n_kv).astype(np.int32)[None, :])
    lane_kv = jnp.asarray((lane % n_kv).astype(np.int32)[None, :])
    per_seq = lambda b, j, pt: (b, 0, 0)
    fixed = lambda b, j, pt: (0, 0)
    blocks = (2 * PAGE_SLOTS * pps * _nbytes((page * n_kv, hd), F32) + pps * _nbytes((page * n_kv, hd), F32)
              + nb * (_nbytes((n_q, hd), F32) + _nbytes((SUBLANES, hd), F32)) + 8 * pps * _nbytes((n_q, page * n_kv), F32))
    kern = functools.partial(_moba_sample_kernel, pages_per_step=pps, base=layer * n_pool, ppb=ppb, nb=nb, group=group,
                             n_kv=n_kv, hd=hd, topk=topk, scale=hd ** -0.5)
    return pl.pallas_call(
        kern,
        grid_spec=pltpu.PrefetchScalarGridSpec(
            num_scalar_prefetch=1,
            grid=(nseq, n_pages // pps),
            in_specs=[
                pl.BlockSpec((1, n_q, hd), per_seq), pl.BlockSpec((1, n_kv, hd), per_seq), pl.BlockSpec((1, n_kv, hd), per_seq),
                pl.BlockSpec((n_q, ppb * page * n_kv), fixed), pl.BlockSpec((1, LANES), fixed), pl.BlockSpec((1, LANES), fixed),
                pl.BlockSpec(memory_space=pl.ANY), pl.BlockSpec(memory_space=pl.ANY),
            ],
            out_specs=pl.BlockSpec((1, n_q, hd), per_seq),
            scratch_shapes=[pltpu.VMEM((n_q, LANES), F32), pltpu.VMEM((n_q, LANES), F32), pltpu.VMEM((nb, n_q, hd), F32),
                            pltpu.VMEM((nb, SUBLANES, hd), F32), pltpu.VMEM((LANES, hd), F32),
                            pltpu.VMEM((PAGE_SLOTS, pps, page * n_kv, hd), F32), pltpu.VMEM((PAGE_SLOTS, pps, page * n_kv, hd), F32),
                            pltpu.SemaphoreType.DMA((PAGE_SLOTS, pps)), pltpu.SemaphoreType.DMA((PAGE_SLOTS, pps))],
        ),
        out_shape=jax.ShapeDtypeStruct((nseq, n_q, hd), F32),
        compiler_params=_params(("arbitrary", "arbitrary"), blocks),
        name="moba_sample",
    )(page_table_flat, q, k_new, v_new, mask, lane_blk, lane_kv, cache_k, cache_v)


def _mla_in_kernel(x_ref, w_ref, qg_ref, kvg_ref, cos_ref, sin_ref, cq_ref, ckv_ref, ckvb_ref, kr_ref, krb_ref, *, q_lora, kv_lora, half):
    h = jnp.dot(x_ref[...], w_ref[...], preferred_element_type=F32)

    def rms(x, g):
        return x * lax.rsqrt(jnp.mean(x * x, axis=-1, keepdims=True) + RMS_EPS) * g

    cq_ref[...] = rms(h[:, :q_lora], qg_ref[...]).astype(BF16)
    ckv = rms(h[:, q_lora:q_lora + kv_lora], kvg_ref[...])
    ckv_ref[...] = ckv
    ckvb_ref[...] = ckv.astype(BF16)
    kr = h[:, q_lora + kv_lora:]
    kr = kr * cos_ref[...] + _rotate_half(kr, half) * sin_ref[...]
    kr_ref[...] = kr
    krb_ref[...] = kr.astype(BF16)


def _mla_in(x, w, q_norm, kv_norm, cos, sin_signed, q_lora, kv_lora, half):
    n, d = x.shape
    wid = w.shape[1]
    tm = _divisor_tile(n, 640, 16)
    blocks = (2 * (_nbytes((tm, d), BF16) + _nbytes((d, wid), BF16) + 2 * _nbytes((tm, LANES), F32)
                   + _nbytes((tm, q_lora), BF16) + _nbytes((tm, kv_lora), F32) + _nbytes((tm, kv_lora), BF16)
                   + _nbytes((tm, LANES), F32) + _nbytes((tm, LANES), BF16)) + 3 * _nbytes((tm, wid), F32))
    row = lambda i: (i, 0)
    fixed = lambda i: (0, 0)
    return pl.pallas_call(
        functools.partial(_mla_in_kernel, q_lora=q_lora, kv_lora=kv_lora, half=half),
        grid=(n // tm,),
        in_specs=[
            pl.BlockSpec((tm, d), row), pl.BlockSpec((d, wid), fixed), pl.BlockSpec((1, q_lora), fixed),
            pl.BlockSpec((1, kv_lora), fixed), pl.BlockSpec((tm, LANES), row), pl.BlockSpec((tm, LANES), row),
        ],
        out_specs=[pl.BlockSpec((tm, q_lora), row), pl.BlockSpec((tm, kv_lora), row), pl.BlockSpec((tm, kv_lora), row),
                   pl.BlockSpec((tm, LANES), row), pl.BlockSpec((tm, LANES), row)],
        out_shape=[jax.ShapeDtypeStruct((n, q_lora), BF16), jax.ShapeDtypeStruct((n, kv_lora), F32),
                   jax.ShapeDtypeStruct((n, kv_lora), BF16), jax.ShapeDtypeStruct((n, LANES), F32),
                   jax.ShapeDtypeStruct((n, LANES), BF16)],
        compiler_params=_params(("parallel",), blocks),
        name="mla_in",
    )(x, w, q_norm, kv_norm, cos, sin_signed)


def _mla_prompt_kernel(qn_ref, qr_ref, kn_ref, kr_ref, v_ref, o_ref, *, scale):
    i = pl.program_id(2)
    tq = qn_ref.shape[0]
    seq = kn_ref.shape[0]
    q = jnp.concatenate([qn_ref[...], qr_ref[...]], axis=1)
    lower = lax.broadcasted_iota(jnp.int32, (tq, tq), 1) <= lax.broadcasted_iota(jnp.int32, (tq, tq), 0)

    def scores(lo, hi):
        k = jnp.concatenate([kn_ref[lo:hi, :], kr_ref[lo:hi, :]], axis=1)
        return lax.dot_general(q, k, NT_DIMS, preferred_element_type=F32) * scale

    def attend(tile):
        past = tile * tq
        s_own = jnp.where(lower, scores(past, past + tq), MASKED)
        m = jnp.max(s_own, axis=-1, keepdims=True)
        if tile:
            s_past = scores(0, past)
            m = jnp.maximum(m, jnp.max(s_past, axis=-1, keepdims=True))
            p_past = jnp.exp(s_past - m)
        p_own = jnp.exp(s_own - m)
        l = jnp.sum(p_own, axis=-1, keepdims=True)
        o = jnp.dot(p_own.astype(BF16), v_ref[past:past + tq, :], preferred_element_type=F32)
        if tile:
            l = l + jnp.sum(p_past, axis=-1, keepdims=True)
            o = o + jnp.dot(p_past.astype(BF16), v_ref[0:past, :], preferred_element_type=F32)
        o_ref[...] = (o / l).astype(o_ref.dtype)

    for tile in range(seq // tq):
        pl.when(i == tile)(functools.partial(attend, tile))


def _mla_prompt(q2, kv, krb, batch, seq, heads, scale):
    tq = _divisor_tile(seq, 256, 16)
    nqb = seq // tq
    blocks = 2 * (3 * _nbytes((tq, LANES), BF16) + 3 * _nbytes((seq, LANES), BF16)) + 6 * _nbytes((tq, seq), F32)
    return pl.pallas_call(
        functools.partial(_mla_prompt_kernel, scale=scale),
        grid=(batch, heads, nqb),
        in_specs=[
            pl.BlockSpec((tq, LANES), lambda b, h, i: (b * nqb + i, h)),
            pl.BlockSpec((tq, LANES), lambda b, h, i: (b * nqb + i, heads + h)),
            pl.BlockSpec((seq, LANES), lambda b, h, i: (b, h)),
            pl.BlockSpec((seq, LANES), lambda b, h, i: (b, 0)),
            pl.BlockSpec((seq, LANES), lambda b, h, i: (b, heads + h)),
        ],
        out_specs=pl.BlockSpec((tq, LANES), lambda b, h, i: (b * nqb + i, h)),
        out_shape=jax.ShapeDtypeStruct((batch * seq, heads * LANES), BF16),
        compiler_params=_params(("parallel", "parallel", "parallel"), blocks),
        name="mla_prompt",
    )(q2, q2, kv, krb, kv)


def _head_mm_nt_kernel(x_ref, w_ref, o_ref):
    o_ref[...] = lax.dot_general(x_ref[...].astype(BF16), w_ref[...].astype(BF16), NT_DIMS, preferred_element_type=F32)


def _head_mm_kernel(x_ref, w_ref, o_ref):
    o_ref[...] = jnp.dot(x_ref[...].astype(BF16), w_ref[...].astype(BF16), preferred_element_type=F32).astype(o_ref.dtype)


def _mla_q_latent(q2, w_uk2d, row_block, nb, heads, kv_lora):
    blocks = 2 * (_nbytes((nb, LANES), BF16) + _nbytes((kv_lora, LANES), F32) + _nbytes((nb, kv_lora), F32)) * 2
    return pl.pallas_call(
        _head_mm_nt_kernel,
        grid=(heads,),
        in_specs=[pl.BlockSpec((nb, LANES), lambda h: (row_block, h)), pl.BlockSpec((kv_lora, LANES), lambda h: (0, h))],
        out_specs=pl.BlockSpec((nb, kv_lora), lambda h: (0, h)),
        out_shape=jax.ShapeDtypeStruct((nb, heads * kv_lora), F32),
        compiler_params=_params(("parallel",), blocks),
        name="mla_q_latent",
    )(q2, w_uk2d)


def _mla_out_latent(o_lat, w_uv2d, heads, kv_lora):
    nb = o_lat.shape[0]
    blocks = 2 * (_nbytes((nb, kv_lora), F32) + _nbytes((kv_lora, LANES), F32) + _nbytes((nb, LANES), F32)) * 2
    return pl.pallas_call(
        _head_mm_kernel,
        grid=(heads,),
        in_specs=[pl.BlockSpec((nb, kv_lora), lambda h: (0, h)), pl.BlockSpec((kv_lora, LANES), lambda h: (0, h))],
        out_specs=pl.BlockSpec((nb, LANES), lambda h: (0, h)),
        out_shape=jax.ShapeDtypeStruct((nb, heads * LANES), BF16),
        compiler_params=_params(("parallel",), blocks),
        name="mla_out_latent",
    )(o_lat, w_uv2d)


def _mla_sample_kernel(pt_ref, qlat_ref, qr_ref, cnew_ref, rnew_ref, c_hbm, r_hbm, o_ref, m_ref, l_ref, acc_ref, cbuf, rbuf, csem, rsem, *, pages_per_step, base, rope, scale):
    j = pl.program_id(1)
    stream = _PageStream(pt_ref, pages_per_step, base, ((c_hbm, cbuf, csem), (r_hbm, rbuf, rsem)))
    slot = stream.begin_step()
    qlat = qlat_ref[0]
    qr = qr_ref[0][:, :rope]

    @pl.when(j == 0)
    def _():
        c_new = cnew_ref[0]
        s_new = (jnp.sum(qlat * c_new, axis=-1, keepdims=True)
                 + jnp.sum(qr.astype(F32) * rnew_ref[0], axis=-1, keepdims=True)) * scale
        m_ref[...] = jnp.broadcast_to(s_new, m_ref.shape)
        l_ref[...] = jnp.ones_like(l_ref)
        acc_ref[...] = jnp.broadcast_to(c_new, acc_ref.shape)

    qlb = qlat.astype(BF16)
    qrb = qr.astype(BF16)
    page = cbuf.shape[2]
    cbs, parts = [], []
    for p in range(pages_per_step):
        stream.prefetch(p)
        cbs.append(cbuf[slot, p].astype(BF16))
        parts.append(lax.dot_general(qlb, cbs[p], NT_DIMS, preferred_element_type=F32)
                     + jnp.dot(qrb, rbuf[slot, p].astype(BF16), preferred_element_type=F32))
    s = jnp.concatenate(parts, axis=-1) * scale
    m_old = m_ref[...][:, 0:1]
    m_new = jnp.maximum(m_old, jnp.max(s, axis=-1, keepdims=True))
    corr = jnp.exp(m_old - m_new)
    e = jnp.exp(s - m_new)
    l_new = l_ref[...][:, 0:1] * corr + jnp.sum(e, axis=-1, keepdims=True)
    eb = e.astype(BF16)
    acc = acc_ref[...] * corr
    for p in range(pages_per_step):
        acc = acc + jnp.dot(eb[:, p * page:(p + 1) * page], cbs[p], preferred_element_type=F32)
    m_ref[...] = jnp.broadcast_to(m_new, m_ref.shape)
    l_ref[...] = jnp.broadcast_to(l_new, l_ref.shape)
    acc_ref[...] = acc

    @pl.when(j == pl.num_programs(1) - 1)
    def _():
        o_ref[0] = acc / l_new

    stream.end_step()


def _mla_sample(qlat, qr, c_new, r_new, lat_cache, rope_cache_t, page_table_flat, layer, n_pool, n_pages, scale):
    nb, heads, kv_lora = qlat.shape
    page = lat_cache.shape[1]
    rope = rope_cache_t.shape[1]
    pps = _divisor_tile(n_pages, 16, 1)
    per_seq = lambda b, j, pt: (b, 0, 0)
    blocks = (PAGE_SLOTS * pps * (_nbytes((page, kv_lora), F32) + _nbytes((rope, page), F32)) + 4 * _nbytes((heads, kv_lora), F32)
              + pps * _nbytes((page, kv_lora), F32))
    kern = functools.partial(_mla_sample_kernel, pages_per_step=pps, base=layer * n_pool, rope=rope, scale=scale)
    return pl.pallas_call(
        kern,
        grid_spec=pltpu.PrefetchScalarGridSpec(
            num_scalar_prefetch=1,
            grid=(nb, n_pages // pps),
            in_specs=[
                pl.BlockSpec((1, heads, kv_lora), per_seq), pl.BlockSpec((1, heads, LANES), per_seq),
                pl.BlockSpec((1, 1, kv_lora), per_seq), pl.BlockSpec((1, 1, rope), per_seq),
                pl.BlockSpec(memory_space=pl.ANY), pl.BlockSpec(memory_space=pl.ANY),
            ],
            out_specs=pl.BlockSpec((1, heads, kv_lora), per_seq),
            scratch_shapes=[pltpu.VMEM((heads, LANES), F32), pltpu.VMEM((heads, LANES), F32), pltpu.VMEM((heads, kv_lora), F32),
                            pltpu.VMEM((PAGE_SLOTS, pps, page, kv_lora), F32), pltpu.VMEM((PAGE_SLOTS, pps, rope, page), F32),
                            pltpu.SemaphoreType.DMA((PAGE_SLOTS, pps)), pltpu.SemaphoreType.DMA((PAGE_SLOTS, pps))],
        ),
        out_shape=jax.ShapeDtypeStruct((nb, heads, kv_lora), F32),
        compiler_params=_params(("arbitrary", "arbitrary"), blocks),
        name="mla_sample",
    )(page_table_flat, qlat, qr, c_new, r_new, lat_cache, rope_cache_t)


def _rope_tables(pos, dim):
    inv = ROPE_THETA ** (-jnp.arange(0, dim, 2, dtype=F32) / dim)
    ang = pos.astype(F32)[:, None] * inv[None, :]
    ang = jnp.concatenate([ang, ang], axis=-1)
    sign = jnp.where(jnp.arange(dim) < dim // 2, -1.0, 1.0).astype(F32)
    cos, sin = jnp.cos(ang), jnp.sin(ang) * sign[None, :]
    if dim < LANES:
        pad = ((0, 0), (0, LANES - dim))
        cos, sin = jnp.pad(cos, pad), jnp.pad(sin, pad)
    return cos, sin


def kernel(x_prompt, x_sample, state_conv, cache_moba_k, cache_moba_v, cache_mla_latent, cache_mla_krope, page_table, conv_w_in, conv_b_in, conv_w_dw, conv_b_dw, conv_ln_g, conv_ln_b, conv_w_out, conv_b_out, moba_w_qkv, moba_w_o, mla_w_in, mla_q_norm, mla_w_uq, mla_kv_norm, mla_w_uk, mla_w_uv, mla_w_o, mlp_w1, mlp_w2, ln_g, ln_b):
    bp, sp, d = x_prompt.shape
    bs, ts, _ = x_sample.shape
    assert ts == 1 and (bp * sp) % bs == 0
    n_prompt = bp * sp
    depth = mlp_w1.shape[0]
    alpha = (2 * depth) ** 0.25
    n_pages = page_table.shape[1]
    n_pool, page, n_kv, hd = cache_moba_k.shape[1:]
    past_len = n_pages * page
    assert past_len % MOBA_BLOCK == 0
    n_q = moba_w_qkv.shape[2] // hd - 2 * n_kv
    q_lora = mla_q_norm.shape[1]
    kv_lora, heads, nope = mla_w_uk.shape[1:]
    v_dim = mla_w_uv.shape[3]
    rope = cache_mla_krope.shape[3]
    conv_width = conv_w_dw.shape[1]
    assert nope == LANES and v_dim == LANES and rope <= LANES and sp >= conv_width - 1
    mla_scale = (nope + rope) ** -0.5

    pos = jnp.concatenate([jnp.tile(jnp.arange(sp, dtype=jnp.int32), bp), jnp.full((bs,), past_len, jnp.int32)])
    cos_moba, sin_moba = _rope_tables(pos, hd)
    cos_mla, sin_mla = _rope_tables(pos, rope)
    page_table_flat = page_table.reshape(-1).astype(jnp.int32)
    row1 = lambda v: v.reshape(1, -1)

    h = jnp.concatenate([x_prompt.reshape(n_prompt, d), x_sample.reshape(bs, d)], axis=0)
    hb = h.astype(BF16)
    mlp_w1_b, mlp_w2_b = mlp_w1.astype(BF16), mlp_w2.astype(BF16)
    conv_w_in_b, conv_w_out_b = conv_w_in.astype(BF16), conv_w_out.astype(BF16)
    moba_w_o_b, mla_w_o_b = moba_w_o.astype(BF16), mla_w_o.astype(BF16)
    no_bias = jnp.zeros((1, d), F32)

    conv_p, conv_s = [], []
    mk_p, mv_p, mk_s, mv_s = [], [], [], []
    ml_p, mr_p, ml_s, mr_s = [], [], [], []
    for i in range(depth):
        kind, j = i % 3, i // 3
        if kind == 0:
            u = _glu_matmul(hb, conv_w_in_b, j, row1(conv_b_in[j]))
            mixed_p = _conv_prompt(u, conv_w_dw[j], row1(conv_b_dw[j]), row1(conv_ln_g[j]), row1(conv_ln_b[j]), bp, sp)
            prev = state_conv[j].astype(F32)
            mixed_s = _conv_sample(prev.transpose(1, 0, 2), u, n_prompt // bs, conv_w_dw[j], row1(conv_b_dw[j]),
                                   row1(conv_ln_g[j]), row1(conv_ln_b[j]))
            conv_p.append(jnp.stack([u[(b + 1) * sp - (conv_width - 1):(b + 1) * sp] for b in range(bp)]))
            conv_s.append(jnp.concatenate([prev[:, 1:], u[n_prompt:].reshape(bs, 1, -1)], axis=1))
            mixed = jnp.concatenate([mixed_p, mixed_s], axis=0)
            w_out, b_out = conv_w_out_b, row1(conv_b_out[j])
        elif kind == 1:
            nq_cols, nk_cols = n_q * hd, n_kv * hd
            qkv = _matmul_rope(hb, moba_w_qkv[j].astype(BF16), cos_moba, sin_moba, F32, (0, nq_cols + nk_cols), hd // 2,
                               tn=math.gcd(nq_cols, nk_cols))
            mixed_p = _moba_prompt(qkv, bp, sp, n_q, n_kv, hd)
            group = n_q // n_kv
            q_s = qkv[n_prompt:, :nq_cols].reshape(bs, n_kv, group, hd).transpose(0, 2, 1, 3).reshape(bs, n_q, hd)
            k_s = qkv[n_prompt:, nq_cols:nq_cols + nk_cols].reshape(bs, n_kv, hd)
            v_s = qkv[n_prompt:, nq_cols + nk_cols:].reshape(bs, n_kv, hd)
            o_s = _moba_sample(q_s, k_s, v_s,
                               cache_moba_k.reshape(-1, page * n_kv, hd), cache_moba_v.reshape(-1, page * n_kv, hd),
                               page_table_flat, j, n_pool, n_pages, page, n_kv)
            mixed_s = o_s.reshape(bs, group, n_kv, hd).transpose(0, 2, 1, 3).reshape(bs, nq_cols).astype(BF16)
            mk_p.append(qkv[:n_prompt, nq_cols:nq_cols + nk_cols].reshape(bp, sp, n_kv, hd))
            mv_p.append(qkv[:n_prompt, nq_cols + nk_cols:].reshape(bp, sp, n_kv, hd))
            mk_s.append(k_s.reshape(bs, 1, n_kv, hd))
            mv_s.append(v_s.reshape(bs, 1, n_kv, hd))
            mixed = jnp.concatenate([mixed_p, mixed_s], axis=0)
            w_out, b_out = moba_w_o_b, no_bias
        else:
            w_in = jnp.pad(mla_w_in[j], ((0, 0), (0, LANES - rope))).astype(BF16)
            cq, ckv, ckv_b, kr, kr_b = _mla_in(hb, w_in, row1(mla_q_norm[j]), row1(mla_kv_norm[j]), cos_mla, sin_mla,
                                               q_lora, kv_lora, rope // 2)
            w_uq = mla_w_uq[j].reshape(q_lora, heads, nope + rope)
            w_uq_rope = jnp.pad(w_uq[:, :, nope:], ((0, 0), (0, 0), (0, LANES - rope)))
            w_uq2 = jnp.concatenate([w_uq[:, :, :nope].reshape(q_lora, heads * nope),
                                     w_uq_rope.reshape(q_lora, heads * LANES)], axis=1).astype(BF16)
            q2 = _matmul_rope(cq, w_uq2, cos_mla, sin_mla, BF16, (heads * nope, heads * (nope + LANES)), rope // 2,
                              tn=_divisor_tile(heads * LANES, 512, LANES))
            w_uk2d = mla_w_uk[j].reshape(kv_lora, heads * nope)
            w_uv2d = mla_w_uv[j].reshape(kv_lora, heads * v_dim)
            kv = _matmul(ckv_b, jnp.concatenate([w_uk2d, w_uv2d], axis=1).astype(BF16), BF16, rows=n_prompt)
            mixed_p = _mla_prompt(q2, kv, kr_b, bp, sp, heads, mla_scale)
            qlat = _mla_q_latent(q2, w_uk2d, n_prompt // bs, bs, heads, kv_lora).reshape(bs, heads, kv_lora)
            qr_s = q2[n_prompt:, heads * nope:].reshape(bs, heads, LANES)
            c_new = ckv[n_prompt:].reshape(bs, 1, kv_lora)
            r_new = kr[n_prompt:, :rope].reshape(bs, 1, rope)
            o_lat = _mla_sample(qlat, qr_s, c_new, r_new,
                                cache_mla_latent.reshape(-1, page, kv_lora),
                                cache_mla_krope.transpose(0, 1, 3, 2).reshape(-1, rope, page),
                                page_table_flat, j, cache_mla_latent.shape[1], n_pages, mla_scale)
            mixed_s = _mla_out_latent(o_lat.reshape(bs, heads * kv_lora), w_uv2d, heads, kv_lora)
            ml_p.append(ckv[:n_prompt].reshape(bp, sp, kv_lora))
            mr_p.append(kr[:n_prompt, :rope].reshape(bp, sp, rope))
            ml_s.append(c_new)
            mr_s.append(r_new)
            mixed = jnp.concatenate([mixed_p, mixed_s], axis=0)
            w_out, b_out = mla_w_o_b, no_bias
        h, hb = _matmul_res_ln(mixed, w_out, j, b_out, h, row1(ln_g[i, 0]), row1(ln_b[i, 0]), alpha)
        h, hb = _mlp_res_ln(hb, h, mlp_w1_b, mlp_w2_b, i, row1(ln_g[i, 1]), row1(ln_b[i, 1]), alpha)

    return (h[:n_prompt].reshape(bp, sp, d), h[n_prompt:].reshape(bs, 1, d),
            jnp.stack(conv_p), jnp.stack(conv_s),
            jnp.stack(mk_p), jnp.stack(mv_p), jnp.stack(mk_s), jnp.stack(mv_s),
            jnp.stack(ml_p), jnp.stack(mr_p), jnp.stack(ml_s), jnp.stack(mr_s))
```
